```python
import math
import jax, jax.numpy as jnp
from jax import lax
import numpy as np


D_MODEL = 4096
BATCH = 1
SEQ = 8192
DEPTH = 4

CHUNK = 64
Q_BLOCK = 128
V_HEAD_DIM = 128
N_HEADS = D_MODEL // V_HEAD_DIM
QK_NOPE_DIM = 128
QK_ROPE_DIM = 64
QK_HEAD_DIM = QK_NOPE_DIM + QK_ROPE_DIM
Q_LORA_RANK = 1024
KV_LORA_RANK = 512
ROPE_THETA = 10000.0
CONV_DIM = D_MODEL
CONV_WIDTH = 3
D_FF = 2 * D_MODEL
DEEPNORM_ALPHA = (2.0 * DEPTH) ** 0.25
DEEPNORM_BETA = (8.0 * DEPTH) ** -0.25
LN_EPS = 1e-5
RMS_EPS = 1e-6

OFF_Q = 0
OFF_KV = OFF_Q + Q_LORA_RANK
OFF_KR = OFF_KV + KV_LORA_RANK
OFF_CB = OFF_KR + QK_ROPE_DIM
OFF_CC = OFF_CB + CONV_DIM
OFF_CH = OFF_CC + CONV_DIM
OFF_GA = OFF_CH + CONV_DIM
OFF_GC = OFF_GA + D_MODEL
N_IN = OFF_GC + D_MODEL

kernel_name = "hybrid_mla_shortconv_convffn_deepnorm"


def rmsnorm(x, w):
    xf = x.astype(jnp.float32)
    y = xf * lax.rsqrt(jnp.mean(xf * xf, axis=-1, keepdims=True) + RMS_EPS)
    return (y * w.astype(jnp.float32)).astype(x.dtype)


def layernorm(x, g, b):
    xf = x.astype(jnp.float32)
    mu = jnp.mean(xf, axis=-1, keepdims=True)
    var = jnp.mean(jnp.square(xf - mu), axis=-1, keepdims=True)
    y = (xf - mu) * lax.rsqrt(var + LN_EPS)
    return (y * g.astype(jnp.float32) + b.astype(jnp.float32)).astype(x.dtype)


def rope_tables(positions):
    inv_freq = 1.0 / (ROPE_THETA ** (jnp.arange(0, QK_ROPE_DIM, 2, dtype=jnp.float32) / QK_ROPE_DIM))
    ang = positions.astype(jnp.float32)[..., None] * inv_freq
    return jnp.cos(ang), jnp.sin(ang)


def apply_rope(x, cos, sin):
    half = x.shape[-1] // 2
    x1, x2 = x[..., :half], x[..., half:]
    c, s = cos.astype(x.dtype), sin.astype(x.dtype)
    return jnp.concatenate([x1 * c - x2 * s, x2 * c + x1 * s], axis=-1)


def causal_dwconv3(u, w):
    S = u.shape[1]
    up = jnp.pad(u, ((0, 0), (CONV_WIDTH - 1, 0), (0, 0)))
    return up[:, 0:S] * w[0] + up[:, 1:S + 1] * w[1] + up[:, 2:S + 2] * w[2]


def mla_branch(c_q, c_kv, k_rope_raw, cos, sin, q_norm_w, kv_norm_w, w_uq, w_ukv):
    B, S, _ = c_q.shape
    q = (rmsnorm(c_q, q_norm_w) @ w_uq).reshape(B, S, N_HEADS, QK_HEAD_DIM)
    q_nope, q_rope = q[..., :QK_NOPE_DIM], q[..., QK_NOPE_DIM:]
    q_rope = apply_rope(q_rope, cos[:, :, None, :], sin[:, :, None, :])
    kv = (rmsnorm(c_kv, kv_norm_w) @ w_ukv).reshape(B, S, N_HEADS, QK_NOPE_DIM + V_HEAD_DIM)
    k_nope, v = kv[..., :QK_NOPE_DIM], kv[..., QK_NOPE_DIM:]
    k_rope = apply_rope(k_rope_raw, cos, sin)

    nb = S // Q_BLOCK
    qn_b = q_nope.reshape(B, nb, Q_BLOCK, N_HEADS, QK_NOPE_DIM).transpose(1, 0, 2, 3, 4)
    qr_b = q_rope.reshape(B, nb, Q_BLOCK, N_HEADS, QK_ROPE_DIM).transpose(1, 0, 2, 3, 4)
    k_chunk = jnp.arange(S) // CHUNK
    scale = 1.0 / math.sqrt(QK_HEAD_DIM)

    def block(args):
        qn, qr, bi = args
        s = (jnp.einsum('bqhd,bkhd->bhqk', qn, k_nope)
             + jnp.einsum('bqhr,bkr->bhqk', qr, k_rope))
        q_chunk = (bi * Q_BLOCK + jnp.arange(Q_BLOCK)) // CHUNK
        mask = k_chunk[None, :] <= q_chunk[:, None]
        s = jnp.where(mask, s.astype(jnp.float32) * scale, -jnp.inf)
        p = jax.nn.softmax(s, axis=-1).astype(v.dtype)
        return jnp.einsum('bhqk,bkhd->bqhd', p, v)

    o = lax.map(block, (qn_b, qr_b, jnp.arange(nb)))
    return o.transpose(1, 0, 2, 3, 4).reshape(B, S, N_HEADS * V_HEAD_DIM)


def setup_inputs(seed: int = 0) -> dict:
    key = jax.random.key(seed)
    ks = jax.random.split(key, 18)
    f32 = jnp.float32
    nrm = lambda k, shape, s: jax.random.normal(k, shape, f32) * s
    x = jax.random.normal(ks[0], (BATCH, SEQ, D_MODEL), f32)
    positions = jnp.broadcast_to(jnp.arange(SEQ, dtype=jnp.int32)[None, :], (BATCH, SEQ))
    return {
        "x": x,
        "positions": positions,
        "w_in": nrm(ks[1], (DEPTH, D_MODEL, N_IN), D_MODEL ** -0.5),
        "b_gate": nrm(ks[2], (DEPTH, 2 * D_MODEL), 0.02),
        "q_norm_w": 1.0 + nrm(ks[3], (DEPTH, Q_LORA_RANK), 0.02),
        "kv_norm_w": 1.0 + nrm(ks[4], (DEPTH, KV_LORA_RANK), 0.02),
        "w_uq": nrm(ks[5], (DEPTH, Q_LORA_RANK, N_HEADS * QK_HEAD_DIM), Q_LORA_RANK ** -0.5),
        "w_ukv": nrm(ks[6], (DEPTH, KV_LORA_RANK, N_HEADS * (QK_NOPE_DIM + V_HEAD_DIM)), KV_LORA_RANK ** -0.5),
        "conv_w": nrm(ks[7], (DEPTH, CONV_WIDTH, CONV_DIM), CONV_WIDTH ** -0.5),
        "w_o": nrm(ks[8], (DEPTH, D_MODEL, D_MODEL), D_MODEL ** -0.5 * DEEPNORM_BETA),
        "ln1_g": 1.0 + nrm(ks[9], (DEPTH, D_MODEL), 0.02),
        "ln1_b": nrm(ks[10], (DEPTH, D_MODEL), 0.02),
        "w_ffn_in": nrm(ks[11], (DEPTH, D_MODEL, 2 * D_FF), D_MODEL ** -0.5),
        "ffn_conv_w": nrm(ks[12], (DEPTH, CONV_WIDTH, D_FF), CONV_WIDTH ** -0.5),
        "w_ffn_down": nrm(ks[13], (DEPTH, D_FF, D_MODEL), D_FF ** -0.5 * DEEPNORM_BETA),
        "ln2_g": 1.0 + nrm(ks[14], (DEPTH, D_MODEL), 0.02),
        "ln2_b": nrm(ks[15], (DEPTH, D_MODEL), 0.02),
    }


def reference(x, positions, w_in, b_gate, q_norm_w, kv_norm_w, w_uq, w_ukv, conv_w, w_o,
              ln1_g, ln1_b, w_ffn_in, ffn_conv_w, w_ffn_down, ln2_g, ln2_b):
    cos, sin = rope_tables(positions)
    for l in range(DEPTH):
        proj = x @ w_in[l]
        c_q = proj[..., OFF_Q:OFF_KV]
        c_kv = proj[..., OFF_KV:OFF_KR]
        k_rope_raw = proj[..., OFF_KR:OFF_CB]
        gate_b = proj[..., OFF_CB:OFF_CC]
        gate_c = proj[..., OFF_CC:OFF_CH]
        h = proj[..., OFF_CH:OFF_GA]
        gates = jax.nn.sigmoid(proj[..., OFF_GA:N_IN] + b_gate[l])
        g_attn, g_conv = gates[..., :D_MODEL], gates[..., D_MODEL:]

        attn = mla_branch(c_q, c_kv, k_rope_raw, cos, sin,
                          q_norm_w[l], kv_norm_w[l], w_uq[l], w_ukv[l])
        conv = gate_b * causal_dwconv3(gate_c * h, conv_w[l])

        mixed = (g_attn * attn + g_conv * conv) @ w_o[l]
        x = layernorm(DEEPNORM_ALPHA * x + mixed, ln1_g[l], ln1_b[l])

        up = x @ w_ffn_in[l]
        a, u = up[..., :D_FF], up[..., D_FF:]
        f = (jax.nn.silu(causal_dwconv3(a, ffn_conv_w[l])) * u) @ w_ffn_down[l]
        x = layernorm(DEEPNORM_ALPHA * x + f, ln2_g[l], ln2_b[l])
    return x
```

```python
import functools
import math

import numpy as np
import jax
import jax.numpy as jnp
from jax import lax
from jax.experimental import pallas as pl
from jax.experimental.pallas import tpu as pltpu

F32 = jnp.float32
BF16 = jnp.bfloat16

CHUNK = 64
V_HEAD_DIM = 128
QK_NOPE_DIM = 128
QK_ROPE_DIM = 64
QK_HEAD_DIM = QK_NOPE_DIM + QK_ROPE_DIM
ROPE_THETA = 10000.0
CONV_WIDTH = 3
LN_EPS = 1e-5
RMS_EPS = 1e-6

V7X_LANES = 128
V7X_SUBLANES = 8
V7X_VMEM_BYTES = 64 * 1024 * 1024
V7X_VMEM_RESERVE = 6 * 1024 * 1024

QK_PAD_DIM = 2 * V7X_LANES


def _vmem_limit(pipelined_bytes, resident_bytes):
    want = 2 * pipelined_bytes + resident_bytes + V7X_VMEM_RESERVE
    return int(min(V7X_VMEM_BYTES - V7X_VMEM_RESERVE, max(want, 16 * 1024 * 1024)))


def _nbytes(shape, dtype):
    return int(np.prod(shape)) * jnp.dtype(dtype).itemsize


def _params(semantics, pipelined_bytes, resident_bytes):
    return pltpu.CompilerParams(dimension_semantics=semantics,
                                vmem_limit_bytes=_vmem_limit(pipelined_bytes, resident_bytes))


def _blk(dim, want):
    b = min(dim, want)
    assert dim % b == 0, (dim, want)
    return b


def _sigmoid(v):
    return 1.0 / (1.0 + jnp.exp(-v))


def _rope_tables_kernel(pos_ref, invf_ref, c_ref, a_ref, b_ref):
    half = QK_ROPE_DIM // 2
    ang = pos_ref[...].astype(F32) * invf_ref[...]
    lane = lax.broadcasted_iota(jnp.int32, ang.shape, 1)
    c = jnp.cos(ang)
    s = jnp.sin(ang)
    c_ref[...] = jnp.where(lane < 2 * half, c, 0.0)
    a_ref[...] = jnp.where(lane < half, -s, 0.0)
    b_ref[...] = jnp.where((lane >= half) & (lane < 2 * half), s, 0.0)


def _rope_tables(positions_col):
    S = positions_col.shape[0]
    half = QK_ROPE_DIM // 2
    inv = (np.float32(1.0) /
           (np.float32(ROPE_THETA) ** (np.arange(0, QK_ROPE_DIM, 2, dtype=np.float32) / np.float32(QK_ROPE_DIM))))
    row = np.zeros((1, V7X_LANES), np.float32)
    row[0, :half] = inv
    row[0, half:2 * half] = inv
    bs = _blk(S, 1024)
    tab = jax.ShapeDtypeStruct((S, V7X_LANES), F32)
    spec = pl.BlockSpec((bs, V7X_LANES), lambda i: (i, 0))
    return pl.pallas_call(
        _rope_tables_kernel,
        grid=(S // bs,),
        in_specs=[pl.BlockSpec((bs, 1), lambda i: (i, 0)),
                  pl.BlockSpec((1, V7X_LANES), lambda i: (0, 0))],
        out_specs=[spec, spec, spec],
        out_shape=[tab, tab, tab],
        name="rope_tables",
    )(positions_col, jnp.asarray(row))


def _rope_tile(r, c, a, b):
    return r * c + pltpu.roll(r, 96, axis=1) * a + pltpu.roll(r, 32, axis=1) * b


def _latent_kernel(x_ref, w_ref, qw_ref, kvw_ref, c_ref, a_ref, b_ref,
                   qn_ref, ckvn_ref, kr_ref, *, q_lora, kv_lora):
    acc = jnp.dot(x_ref[...], w_ref[...], preferred_element_type=F32)

    def rms(v, w):
        return v * lax.rsqrt(jnp.mean(v * v, axis=-1, keepdims=True) + RMS_EPS) * w

    qn_ref[...] = rms(acc[:, :q_lora], qw_ref[...]).astype(BF16)
    ckvn_ref[...] = rms(acc[:, q_lora:q_lora + kv_lora], kvw_ref[...]).astype(BF16)
    kr = acc[:, q_lora + kv_lora:]
    kr_ref[...] = _rope_tile(kr, c_ref[...], a_ref[...], b_ref[...]).astype(BF16)


def _latent_proj(xb, w_lat, q_norm_w, kv_norm_w, tabs):
    S, D = xb.shape
    q_lora = q_norm_w.shape[-1]
    kv_lora = kv_norm_w.shape[-1]
    n_pad = w_lat.shape[-1]
    assert n_pad == q_lora + kv_lora + V7X_LANES
    bm = _blk(S, 512)
    row = lambda i: (i, 0)
    const = lambda i: (0, 0)
    tab_spec = pl.BlockSpec((bm, V7X_LANES), row)
    pipelined = (_nbytes((bm, D), BF16) + _nbytes((D, n_pad), BF16)
                 + 3 * _nbytes((bm, V7X_LANES), F32) + _nbytes((bm, n_pad), BF16))
    return pl.pallas_call(
        functools.partial(_latent_kernel, q_lora=q_lora, kv_lora=kv_lora),
        grid=(S // bm,),
        in_specs=[pl.BlockSpec((bm, D), row),
                  pl.BlockSpec((D, n_pad), const),
                  pl.BlockSpec((1, q_lora), const),
                  pl.BlockSpec((1, kv_lora), const),
                  tab_spec, tab_spec, tab_spec],
        out_specs=[pl.BlockSpec((bm, q_lora), row),
                   pl.BlockSpec((bm, kv_lora), row),
                   pl.BlockSpec((bm, V7X_LANES), row)],
        out_shape=[jax.ShapeDtypeStruct((S, q_lora), BF16),
                   jax.ShapeDtypeStruct((S, kv_lora), BF16),
                   jax.ShapeDtypeStruct((S, V7X_LANES), BF16)],
        compiler_params=_params(("arbitrary",), pipelined, 3 * _nbytes((bm, n_pad), F32)),
        name="latent_proj",
    )(xb, w_lat, q_norm_w.reshape(1, -1), kv_norm_w.reshape(1, -1), *tabs)


def _causal_conv3(z, w_ref, tail_ref, first_tile):
    bm, bn = z.shape

    @pl.when(first_tile)
    def _():
        tail_ref[...] = jnp.zeros_like(tail_ref)

    prev1 = tail_ref[V7X_SUBLANES - 1:V7X_SUBLANES, :]
    prev2 = tail_ref[V7X_SUBLANES - 2:V7X_SUBLANES - 1, :]
    row = lax.broadcasted_iota(jnp.int32, (bm, bn), 0)
    z1 = jnp.where(row == 0, prev1, pltpu.roll(z, 1, axis=0))
    z2 = pltpu.roll(z, 2, axis=0)
    z2 = jnp.where(row == 0, prev2, jnp.where(row == 1, prev1, z2))
    tail_ref[...] = z[bm - V7X_SUBLANES:, :]
    return z2 * w_ref[0:1, :] + z1 * w_ref[1:2, :] + z * w_ref[2:3, :]


def _branch_kernel(x_ref, wcb_ref, wcc_ref, wch_ref, wga_ref, wgc_ref, ba_ref, bc_ref, cw_ref,
                   ct_ref, ga_ref, tail_ref):
    x = x_ref[...]
    dot = lambda w_ref: jnp.dot(x, w_ref[...], preferred_element_type=F32)
    z = dot(wcc_ref) * dot(wch_ref)
    conv = _causal_conv3(z, cw_ref, tail_ref, pl.program_id(1) == 0)
    ct_ref[...] = (_sigmoid(dot(wgc_ref) + bc_ref[...]) * dot(wcb_ref) * conv).astype(BF16)
    ga_ref[...] = _sigmoid(dot(wga_ref) + ba_ref[...]).astype(BF16)


def _branch_proj(xb, w_br, b_gate, conv_w):
    S, D = xb.shape
    C = conv_w.shape[-1]
    assert w_br.shape == (D, 5 * C) and b_gate.shape == (1, 2 * C)
    bm = _blk(S, 1024)
    bn = _blk(C, 256)
    nt = C // bn
    wspec = lambda g: pl.BlockSpec((D, bn), lambda n, m, g=g: (0, g * nt + n))
    out_spec = pl.BlockSpec((bm, bn), lambda n, m: (m, n))
    pipelined = (_nbytes((bm, D), BF16) + 5 * _nbytes((D, bn), BF16) + 2 * _nbytes((bm, bn), BF16))
    return pl.pallas_call(
        _branch_kernel,
        grid=(nt, S // bm),
        in_specs=[pl.BlockSpec((bm, D), lambda n, m: (m, 0)),
                  wspec(0), wspec(1), wspec(2), wspec(3), wspec(4),
                  pl.BlockSpec((1, bn), lambda n, m: (0, n)),
                  pl.BlockSpec((1, bn), lambda n, m: (0, nt + n)),
                  pl.BlockSpec((CONV_WIDTH, bn), lambda n, m: (0, n))],
        out_specs=[out_spec, out_spec],
        out_shape=[jax.ShapeDtypeStruct((S, C), BF16), jax.ShapeDtypeStruct((S, C), BF16)],
        scratch_shapes=[pltpu.VMEM((V7X_SUBLANES, bn), F32)],
        compiler_params=_params(("arbitrary", "arbitrary"), pipelined, 10 * _nbytes((bm, bn), F32)),
        name="branch_proj",
    )(xb, w_br, w_br, w_br, w_br, w_br, b_gate, b_gate, conv_w)


def _qup_kernel(x_ref, w_ref, c_ref, a_ref, b_ref, o_ref, *, scale):
    acc = jnp.dot(x_ref[...], w_ref[...], preferred_element_type=F32)
    c, a, b = c_ref[...], a_ref[...], b_ref[...]
    for h in range(acc.shape[1] // QK_PAD_DIM):
        lo = h * QK_PAD_DIM
        mid = lo + QK_NOPE_DIM
        o_ref[:, lo:mid] = (acc[:, lo:mid] * scale).astype(BF16)
        o_ref[:, mid:lo + QK_PAD_DIM] = (_rope_tile(acc[:, mid:lo + QK_PAD_DIM], c, a, b) * scale).astype(BF16)


def _q_up(qn, w_uq_pad, tabs):
    S, K = qn.shape
    N = w_uq_pad.shape[-1]
    bm = _blk(S, 1024)
    bn = _blk(N, 1024)
    tab_spec = pl.BlockSpec((bm, V7X_LANES), lambda m, n: (m, 0))
    pipelined = (_nbytes((bm, K), BF16) + _nbytes((K, bn), BF16) + 3 * _nbytes((bm, V7X_LANES), F32)
                 + _nbytes((bm, bn), BF16))
    return pl.pallas_call(
        functools.partial(_qup_kernel, scale=1.0 / math.sqrt(QK_HEAD_DIM)),
        grid=(S // bm, N // bn),
        in_specs=[pl.BlockSpec((bm, K), lambda m, n: (m, 0)),
                  pl.BlockSpec((K, bn), lambda m, n: (0, n)),
                  tab_spec, tab_spec, tab_spec],
        out_specs=pl.BlockSpec((bm, bn), lambda m, n: (m, n)),
        out_shape=jax.ShapeDtypeStruct((S, N), BF16),
        compiler_params=_params(("arbitrary", "arbitrary"), pipelined, 2 * _nbytes((bm, bn), F32)),
        name="q_up",
    )(qn, w_uq_pad, *tabs)


def _kvup_kernel(x_ref, w_ref, kr_ref, k_ref, v_ref):
    acc = jnp.dot(x_ref[...], w_ref[...], preferred_element_type=F32)
    kr = kr_ref[...]
    hw = QK_NOPE_DIM + V_HEAD_DIM
    for h in range(acc.shape[1] // hw):
        k_ref[:, h * QK_PAD_DIM:h * QK_PAD_DIM + QK_NOPE_DIM] = acc[:, h * hw:h * hw + QK_NOPE_DIM].astype(BF16)
        k_ref[:, h * QK_PAD_DIM + QK_NOPE_DIM:(h + 1) * QK_PAD_DIM] = kr
        v_ref[:, h * V_HEAD_DIM:(h + 1) * V_HEAD_DIM] = acc[:, h * hw + QK_NOPE_DIM:(h + 1) * hw].astype(BF16)


def _kv_up(ckvn, w_ukv, kr):
    S, K = ckvn.shape
    N = w_ukv.shape[-1]
    hw = QK_NOPE_DIM + V_HEAD_DIM
    H = N // hw
    bm = _blk(S, 1024)
    bn = _blk(N, 1024)
    hb = bn // hw
    pipelined = (_nbytes((bm, K), BF16) + _nbytes((K, bn), BF16) + _nbytes((bm, V7X_LANES), BF16)
                 + _nbytes((bm, hb * (QK_PAD_DIM + V_HEAD_DIM)), BF16))
    return pl.pallas_call(
        _kvup_kernel,
        grid=(S // bm, N // bn),
        in_specs=[pl.BlockSpec((bm, K), lambda m, n: (m, 0)),
                  pl.BlockSpec((K, bn), lambda m, n: (0, n)),
                  pl.BlockSpec((bm, V7X_LANES), lambda m, n: (m, 0))],
        out_specs=[pl.BlockSpec((bm, hb * QK_PAD_DIM), lambda m, n: (m, n)),
                   pl.BlockSpec((bm, hb * V_HEAD_DIM), lambda m, n: (m, n))],
        out_shape=[jax.ShapeDtypeStruct((S, H * QK_PAD_DIM), BF16),
                   jax.ShapeDtypeStruct((S, H * V_HEAD_DIM), BF16)],
        compiler_params=_params(("arbitrary", "arbitrary"), pipelined, 2 * _nbytes((bm, bn), F32)),
        name="kv_up",
    )(ckvn, w_ukv, kr)


def _attn_kernel(q_ref, k_ref, v_ref, ga_ref, ct_ref, o_ref, m_sc, l_sc, acc_sc, *, blk):
    qi = pl.program_id(1)
    q = q_ref[...]
    m_sc[...] = jnp.full(m_sc.shape, -jnp.inf, F32)
    l_sc[...] = jnp.zeros(l_sc.shape, F32)
    acc_sc[...] = jnp.zeros(acc_sc.shape, F32)

    def step(j, mask):
        off = pl.multiple_of(j * blk, blk)
        k = k_ref[pl.ds(off, blk), :]
        v = v_ref[pl.ds(off, blk), :]
        s = lax.dot_general(q, k, (((1,), (1,)), ((), ())), preferred_element_type=F32)
        if mask is not None:
            s = jnp.where(mask, s, -jnp.inf)
        m_prev = m_sc[...]
        m_new = jnp.maximum(m_prev, jnp.max(s, axis=-1, keepdims=True))
        alpha = jnp.exp(m_prev - m_new)
        p = jnp.exp(s - m_new)
        l_sc[...] = alpha * l_sc[...] + jnp.sum(p, axis=-1, keepdims=True)
        acc_sc[...] = alpha * acc_sc[...] + jnp.dot(p.astype(BF16), v, preferred_element_type=F32)
        m_sc[...] = m_new

    def body(j, carry):
        step(j, None)
        return carry

    lax.fori_loop(0, qi, body, 0)
    shift = CHUNK.bit_length() - 1
    r = lax.broadcasted_iota(jnp.int32, (blk, blk), 0) >> shift
    c = lax.broadcasted_iota(jnp.int32, (blk, blk), 1) >> shift
    step(qi, c <= r)
    attn = acc_sc[...] / l_sc[...]
    o_ref[...] = (ga_ref[...].astype(F32) * attn + ct_ref[...].astype(F32)).astype(BF16)


def _attention(q, k, v, g_attn, conv_term):
    S = q.shape[0]
    H = q.shape[1] // QK_PAD_DIM
    blk = _blk(S, 512)
    assert blk % CHUNK == 0 and CHUNK & (CHUNK - 1) == 0
    hv = pl.BlockSpec((blk, V_HEAD_DIM), lambda h, i: (i, h))
    pipelined = (_nbytes((blk, QK_PAD_DIM), BF16) + _nbytes((S, QK_PAD_DIM), BF16) + _nbytes((S, V_HEAD_DIM), BF16)
                 + 3 * _nbytes((blk, V_HEAD_DIM), BF16))
    return pl.pallas_call(
        functools.partial(_attn_kernel, blk=blk),
        grid=(H, S // blk),
        in_specs=[pl.BlockSpec((blk, QK_PAD_DIM), lambda h, i: (i, h)),
                  pl.BlockSpec((S, QK_PAD_DIM), lambda h, i: (0, h)),
                  pl.BlockSpec((S, V_HEAD_DIM), lambda h, i: (0, h)),
                  hv, hv],
        out_specs=hv,
        out_shape=jax.ShapeDtypeStruct((S, H * V_HEAD_DIM), BF16),
        scratch_shapes=[pltpu.VMEM((blk, 1), F32), pltpu.VMEM((blk, 1), F32),
                        pltpu.VMEM((blk, V_HEAD_DIM), F32)],
        compiler_params=_params(("arbitrary", "arbitrary"), pipelined, 6 * _nbytes((blk, blk), F32)),
        name="attention",
    )(q, k, v, g_attn, conv_term)


def _mm_kernel(x_ref, w_ref, o_ref, acc_ref):
    kk = pl.program_id(2)
    part = jnp.dot(x_ref[...], w_ref[...], preferred_element_type=F32)

    @pl.when(kk == 0)
    def _():
        acc_ref[...] = part

    @pl.when(kk > 0)
    def _():
        acc_ref[...] += part

    @pl.when(kk == pl.num_programs(2) - 1)
    def _():
        o_ref[...] = acc_ref[...]


def _matmul(x, w):
    M, K = x.shape
    N = w.shape[-1]
    bm, bn, bk = _blk(M, 1024), _blk(N, 1024), _blk(K, 2048)
    pipelined = _nbytes((bm, bk), BF16) + _nbytes((bk, bn), BF16) + _nbytes((bm, bn), F32)
    return pl.pallas_call(
        _mm_kernel,
        grid=(M // bm, N // bn, K // bk),
        in_specs=[pl.BlockSpec((bm, bk), lambda m, n, k: (m, k)),
                  pl.BlockSpec((bk, bn), lambda m, n, k: (k, n))],
        out_specs=pl.BlockSpec((bm, bn), lambda m, n, k: (m, n)),
        out_shape=jax.ShapeDtypeStruct((M, N), F32),
        scratch_shapes=[pltpu.VMEM((bm, bn), F32)],
        compiler_params=_params(("arbitrary", "arbitrary", "arbitrary"), pipelined, 2 * _nbytes((bm, bn), F32)),
        name="matmul",
    )(x, w)


def _ln_kernel(x_ref, f_ref, g_ref, b_ref, o32_ref, o16_ref, *, alpha):
    y = alpha * x_ref[...] + f_ref[...]
    mu = jnp.mean(y, axis=-1, keepdims=True)
    d = y - mu
    var = jnp.mean(d * d, axis=-1, keepdims=True)
    out = d * lax.rsqrt(var + LN_EPS) * g_ref[...] + b_ref[...]
    o32_ref[...] = out
    o16_ref[...] = out.astype(BF16)


def _deepnorm_ln(x, f, g, b, alpha):
    S, D = x.shape
    bm = _blk(S, 256)
    row = pl.BlockSpec((bm, D), lambda i: (i, 0))
    vec = pl.BlockSpec((1, D), lambda i: (0, 0))
    pipelined = 3 * _nbytes((bm, D), F32) + _nbytes((bm, D), BF16)
    return pl.pallas_call(
        functools.partial(_ln_kernel, alpha=alpha),
        grid=(S // bm,),
        in_specs=[row, row, vec, vec],
        out_specs=[row, row],
        out_shape=[jax.ShapeDtypeStruct((S, D), F32), jax.ShapeDtypeStruct((S, D), BF16)],
        compiler_params=_params(("arbitrary",), pipelined, 3 * _nbytes((bm, D), F32)),
        name="deepnorm_ln",
    )(x, f, g.reshape(1, -1), b.reshape(1, -1))


def _ffn_in_kernel(x_ref, wa_ref, wu_ref, cw_ref, o_ref, tail_ref):
    x = x_ref[...]
    a = jnp.dot(x, wa_ref[...], preferred_element_type=F32)
    conv = _causal_conv3(a, cw_ref, tail_ref, pl.program_id(1) == 0)
    u = jnp.dot(x, wu_ref[...], preferred_element_type=F32)
    o_ref[...] = (conv * _sigmoid(conv) * u).astype(BF16)


def _ffn_in(xb, w_ffn_in, ffn_conv_w):
    S, D = xb.shape
    FF = ffn_conv_w.shape[-1]
    assert w_ffn_in.shape == (D, 2 * FF)
    bm = _blk(S, 1024)
    bn = _blk(FF, 512)
    nt = FF // bn
    pipelined = _nbytes((bm, D), BF16) + 2 * _nbytes((D, bn), BF16) + _nbytes((bm, bn), BF16)
    return pl.pallas_call(
        _ffn_in_kernel,
        grid=(nt, S // bm),
        in_specs=[pl.BlockSpec((bm, D), lambda n, m: (m, 0)),
                  pl.BlockSpec((D, bn), lambda n, m: (0, n)),
                  pl.BlockSpec((D, bn), lambda n, m: (0, nt + n)),
                  pl.BlockSpec((CONV_WIDTH, bn), lambda n, m: (0, n))],
        out_specs=pl.BlockSpec((bm, bn), lambda n, m: (m, n)),
        out_shape=jax.ShapeDtypeStruct((S, FF), BF16),
        scratch_shapes=[pltpu.VMEM((V7X_SUBLANES, bn), F32)],
        compiler_params=_params(("arbitrary", "arbitrary"), pipelined, 6 * _nbytes((bm, bn), F32)),
        name="ffn_in",
    )(xb, w_ffn_in, w_ffn_in, ffn_conv_w)


def kernel(x, positions, w_in, b_gate, q_norm_w, kv_norm_w, w_uq, w_ukv, conv_w, w_o, ln1_g, ln1_b,
           w_ffn_in, ffn_conv_w, w_ffn_down, ln2_g, ln2_b):
    B, S, D = x.shape
    assert B == 1, "kernels are written for a single sequence"
    depth = w_in.shape[0]
    q_lora = q_norm_w.shape[-1]
    kv_lora = kv_norm_w.shape[-1]
    n_lat = q_lora + kv_lora + QK_ROPE_DIM
    assert (q_lora + kv_lora) % V7X_LANES == 0
    H = w_uq.shape[-1] // QK_HEAD_DIM
    alpha = (2.0 * depth) ** 0.25

    w_lat = jnp.pad(w_in[:, :, :n_lat], ((0, 0), (0, 0), (0, V7X_LANES - QK_ROPE_DIM))).astype(BF16)
    w_br = w_in[:, :, n_lat:].astype(BF16)
    w_uq_pad = jnp.pad(w_uq.reshape(depth, q_lora, H, QK_HEAD_DIM),
                       ((0, 0), (0, 0), (0, 0), (0, QK_PAD_DIM - QK_HEAD_DIM))
                       ).reshape(depth, q_lora, H * QK_PAD_DIM).astype(BF16)
    w_ukv_b = w_ukv.astype(BF16)
    w_o_b = w_o.astype(BF16)
    w_ffn_in_b = w_ffn_in.astype(BF16)
    w_ffn_down_b = w_ffn_down.astype(BF16)

    tabs = _rope_tables(positions.reshape(S, 1))
    xf = x.reshape(S, D)
    xb = xf.astype(BF16)
    for l in range(depth):
        qn, ckvn, kr = _latent_proj(xb, w_lat[l], q_norm_w[l], kv_norm_w[l], tabs)
        conv_term, g_attn = _branch_proj(xb, w_br[l], b_gate[l].reshape(1, -1), conv_w[l])
        q = _q_up(qn, w_uq_pad[l], tabs)
        k, v = _kv_up(ckvn, w_ukv_b[l], kr)
        merged = _attention(q, k, v, g_attn, conv_term)
        xf, xb = _deepnorm_ln(xf, _matmul(merged, w_o_b[l]), ln1_g[l], ln1_b[l], alpha)
        g = _ffn_in(xb, w_ffn_in_b[l], ffn_conv_w[l])
        xf, xb = _deepnorm_ln(xf, _matmul(g, w_ffn_down_b[l]), ln2_g[l], ln2_b[l], alpha)
    return xf.reshape(B, S, D)
```

```python
import functools
import math

import numpy as np
import jax
import jax.numpy as jnp
from jax import lax
from jax.experimental import pallas as pl
from jax.experimental.pallas import tpu as pltpu

F32 = jnp.float32
BF16 = jnp.bfloat16

CHUNK = 64
V_HEAD_DIM = 128
QK_NOPE_DIM = 128
QK_ROPE_DIM = 64
QK_HEAD_DIM = QK_NOPE_DIM + QK_ROPE_DIM
ROPE_THETA = 10000.0
CONV_WIDTH = 3
LN_EPS = 1e-5
RMS_EPS = 1e-6

V7X_LANES = 128
V7X_SUBLANES = 8
V7X_VMEM_BYTES = 64 * 1024 * 1024
V7X_VMEM_RESERVE = 6 * 1024 * 1024

QK_PAD_DIM = 2 * V7X_LANES


def _vmem_limit(pipelined_bytes, resident_bytes):
    want = 2 * pipelined_bytes + resident_bytes + V7X_VMEM_RESERVE
    return int(min(V7X_VMEM_BYTES - V7X_VMEM_RESERVE, max(want, 16 * 1024 * 1024)))


def _nbytes(shape, dtype):
    return int(np.prod(shape)) * jnp.dtype(dtype).itemsize


def _params(semantics, pipelined_bytes, resident_bytes):
    return pltpu.CompilerParams(dimension_semantics=semantics,
                                vmem_limit_bytes=_vmem_limit(pipelined_bytes, resident_bytes))


def _blk(dim, want):
    b = min(dim, want)
    assert dim % b == 0, (dim, want)
    return b


def _sigmoid(v):
    return 1.0 / (1.0 + jnp.exp(-v))


def _rope_tables_kernel(pos_ref, invf_ref, c_ref, a_ref, b_ref):
    half = QK_ROPE_DIM // 2
    ang = pos_ref[...].astype(F32) * invf_ref[...]
    lane = lax.broadcasted_iota(jnp.int32, ang.shape, 1)
    c = jnp.cos(ang)
    s = jnp.sin(ang)
    c_ref[...] = jnp.where(lane < 2 * half, c, 0.0)
    a_ref[...] = jnp.where(lane < half, -s, 0.0)
    b_ref[...] = jnp.where((lane >= half) & (lane < 2 * half), s, 0.0)


def _rope_tables(positions_col):
    S = positions_col.shape[0]
    half = QK_ROPE_DIM // 2
    inv = (np.float32(1.0) /
           (np.float32(ROPE_THETA) ** (np.arange(0, QK_ROPE_DIM, 2, dtype=np.float32) / np.float32(QK_ROPE_DIM))))
    row = np.zeros((1, V7X_LANES), np.float32)
    row[0, :half] = inv
    row[0, half:2 * half] = inv
    bs = _blk(S, 1024)
    tab = jax.ShapeDtypeStruct((S, V7X_LANES), F32)
    spec = pl.BlockSpec((bs, V7X_LANES), lambda i: (i, 0))
    return pl.pallas_call(
        _rope_tables_kernel,
        grid=(S // bs,),
        in_specs=[pl.BlockSpec((bs, 1), lambda i: (i, 0)),
                  pl.BlockSpec((1, V7X_LANES), lambda i: (0, 0))],
        out_specs=[spec, spec, spec],
        out_shape=[tab, tab, tab],
        name="rope_tables",
    )(positions_col, jnp.asarray(row))


def _rope_tile(r, c, a, b):
    return r * c + pltpu.roll(r, 96, axis=1) * a + pltpu.roll(r, 32, axis=1) * b


def _latent_kernel(x_ref, w_ref, qw_ref, kvw_ref, c_ref, a_ref, b_ref,
                   qn_ref, ckvn_ref, kr_ref, *, q_lora, kv_lora):
    acc = jnp.dot(x_ref[...], w_ref[...], preferred_element_type=F32)

    def rms(v, w):
        return v * lax.rsqrt(jnp.mean(v * v, axis=-1, keepdims=True) + RMS_EPS) * w

    qn_ref[...] = rms(acc[:, :q_lora], qw_ref[...]).astype(BF16)
    ckvn_ref[...] = rms(acc[:, q_lora:q_lora + kv_lora], kvw_ref[...]).astype(BF16)
    kr = acc[:, q_lora + kv_lora:]
    kr_ref[...] = _rope_tile(kr, c_ref[...], a_ref[...], b_ref[...]).astype(BF16)


def _latent_proj(xb, w_lat, q_norm_w, kv_norm_w, tabs):
    S, D = xb.shape
    q_lora = q_norm_w.shape[-1]
    kv_lora = kv_norm_w.shape[-1]
    n_pad = w_lat.shape[-1]
    assert n_pad == q_lora + kv_lora + V7X_LANES
    bm = _blk(S, 512)
    row = lambda i: (i, 0)
    const = lambda i: (0, 0)
    tab_spec = pl.BlockSpec((bm, V7X_LANES), row)
    pipelined = (_nbytes((bm, D), BF16) + _nbytes((D, n_pad), BF16)
                 + 3 * _nbytes((bm, V7X_LANES), F32) + _nbytes((bm, n_pad), BF16))
    return pl.pallas_call(
        functools.partial(_latent_kernel, q_lora=q_lora, kv_lora=kv_lora),
        grid=(S // bm,),
        in_specs=[pl.BlockSpec((bm, D), row),
                  pl.BlockSpec((D, n_pad), const),
                  pl.BlockSpec((1, q_lora), const),
                  pl.BlockSpec((1, kv_lora), const),
                  tab_spec, tab_spec, tab_spec],
        out_specs=[pl.BlockSpec((bm, q_lora), row),
                   pl.BlockSpec((bm, kv_lora), row),
                   pl.BlockSpec((bm, V7X_LANES), row)],
        out_shape=[jax.ShapeDtypeStruct((S, q_lora), BF16),
                   jax.ShapeDtypeStruct((S, kv_lora), BF16),
                   jax.ShapeDtypeStruct((S, V7X_LANES), BF16)],
        compiler_params=_params(("arbitrary",), pipelined, 3 * _nbytes((bm, n_pad), F32)),
        name="latent_proj",
    )(xb, w_lat, q_norm_w.reshape(1, -1), kv_norm_w.reshape(1, -1), *tabs)


def _causal_conv3(z, w_ref, tail_ref, first_tile):
    bm, bn = z.shape

    @pl.when(first_tile)
    def _():
        tail_ref[...] = jnp.zeros_like(tail_ref)

    prev1 = tail_ref[V7X_SUBLANES - 1:V7X_SUBLANES, :]
    prev2 = tail_ref[V7X_SUBLANES - 2:V7X_SUBLANES - 1, :]
    row = lax.broadcasted_iota(jnp.int32, (bm, bn), 0)
    z1 = jnp.where(row == 0, prev1, pltpu.roll(z, 1, axis=0))
    z2 = pltpu.roll(z, 2, axis=0)
    z2 = jnp.where(row == 0, prev2, jnp.where(row == 1, prev1, z2))
    tail_ref[...] = z[bm - V7X_SUBLANES:, :]
    return z2 * w_ref[0:1, :] + z1 * w_ref[1:2, :] + z * w_ref[2:3, :]


def _group_tiles(w, groups, bn):
    depth, D, GC = w.shape
    nt = GC // groups // bn
    return w.reshape(depth, D, groups, nt, bn).transpose(0, 3, 1, 2, 4).reshape(depth, nt, D, groups * bn)


BRANCH_GROUPS = 5
BRANCH_BN = 256
BRANCH_BM = 512


def _branch_kernel(x_ref, w_ref, ba_ref, bc_ref, cw_ref, ct_ref, ga_ref, tail_ref):
    bn = ct_ref.shape[1]
    acc = jnp.dot(x_ref[...], w_ref[...], preferred_element_type=F32)
    cb, cc, ch, ga, gc = (acc[:, g * bn:(g + 1) * bn] for g in range(BRANCH_GROUPS))
    conv = _causal_conv3(cc * ch, cw_ref, tail_ref, pl.program_id(1) == 0)
    ct_ref[...] = (_sigmoid(gc + bc_ref[...]) * cb * conv).astype(BF16)
    ga_ref[...] = _sigmoid(ga + ba_ref[...]).astype(BF16)


def _branch_proj(xb, w_br, b_gate, conv_w):
    S, D = xb.shape
    C = conv_w.shape[-1]
    nt, _, wn = w_br.shape
    bn = wn // BRANCH_GROUPS
    assert nt * bn == C and b_gate.shape == (1, 2 * C)
    bm = _blk(S, BRANCH_BM)
    out_spec = pl.BlockSpec((bm, bn), lambda n, m: (m, n))
    pipelined = (_nbytes((bm, D), BF16) + _nbytes((D, wn), BF16) + 2 * _nbytes((bm, bn), BF16))
    return pl.pallas_call(
        _branch_kernel,
        grid=(nt, S // bm),
        in_specs=[pl.BlockSpec((bm, D), lambda n, m: (m, 0)),
                  pl.BlockSpec((None, D, wn), lambda n, m: (n, 0, 0)),
                  pl.BlockSpec((1, bn), lambda n, m: (0, n)),
                  pl.BlockSpec((1, bn), lambda n, m: (0, nt + n)),
                  pl.BlockSpec((CONV_WIDTH, bn), lambda n, m: (0, n))],
        out_specs=[out_spec, out_spec],
        out_shape=[jax.ShapeDtypeStruct((S, C), BF16), jax.ShapeDtypeStruct((S, C), BF16)],
        scratch_shapes=[pltpu.VMEM((V7X_SUBLANES, bn), F32)],
        compiler_params=_params(("arbitrary", "arbitrary"), pipelined, 3 * _nbytes((bm, wn), F32)),
        name="branch_proj",
    )(xb, w_br, b_gate, b_gate, conv_w)


def _qup_kernel(x_ref, w_ref, c_ref, a_ref, b_ref, o_ref, *, scale):
    acc = jnp.dot(x_ref[...], w_ref[...], preferred_element_type=F32)
    c, a, b = c_ref[...], a_ref[...], b_ref[...]
    for h in range(acc.shape[1] // QK_PAD_DIM):
        lo = h * QK_PAD_DIM
        mid = lo + QK_NOPE_DIM
        o_ref[:, lo:mid] = (acc[:, lo:mid] * scale).astype(BF16)
        o_ref[:, mid:lo + QK_PAD_DIM] = (_rope_tile(acc[:, mid:lo + QK_PAD_DIM], c, a, b) * scale).astype(BF16)


def _q_up(qn, w_uq_pad, tabs):
    S, K = qn.shape
    N = w_uq_pad.shape[-1]
    bm = _blk(S, 1024)
    bn = _blk(N, 1024)
    tab_spec = pl.BlockSpec((bm, V7X_LANES), lambda m, n: (m, 0))
    pipelined = (_nbytes((bm, K), BF16) + _nbytes((K, bn), BF16) + 3 * _nbytes((bm, V7X_LANES), F32)
                 + _nbytes((bm, bn), BF16))
    return pl.pallas_call(
        functools.partial(_qup_kernel, scale=math.log2(math.e) / math.sqrt(QK_HEAD_DIM)),
        grid=(S // bm, N // bn),
        in_specs=[pl.BlockSpec((bm, K), lambda m, n: (m, 0)),
                  pl.BlockSpec((K, bn), lambda m, n: (0, n)),
                  tab_spec, tab_spec, tab_spec],
        out_specs=pl.BlockSpec((bm, bn), lambda m, n: (m, n)),
        out_shape=jax.ShapeDtypeStruct((S, N), BF16),
        compiler_params=_params(("arbitrary", "arbitrary"), pipelined, 2 * _nbytes((bm, bn), F32)),
        name="q_up",
    )(qn, w_uq_pad, *tabs)


def _kvup_kernel(x_ref, w_ref, kr_ref, k_ref, vt_ref):
    acc = jnp.dot(x_ref[...], w_ref[...], preferred_element_type=F32)
    kr = kr_ref[...]
    hw = QK_NOPE_DIM + V_HEAD_DIM
    for h in range(acc.shape[1] // hw):
        k_ref[:, h * QK_PAD_DIM:h * QK_PAD_DIM + QK_NOPE_DIM] = acc[:, h * hw:h * hw + QK_NOPE_DIM].astype(BF16)
        k_ref[:, h * QK_PAD_DIM + QK_NOPE_DIM:(h + 1) * QK_PAD_DIM] = kr
        vt_ref[h] = acc[:, h * hw + QK_NOPE_DIM:(h + 1) * hw].T.astype(BF16)


def _kv_up(ckvn, w_ukv, kr, blk):
    S, K = ckvn.shape
    N = w_ukv.shape[-1]
    hw = QK_NOPE_DIM + V_HEAD_DIM
    H = N // hw
    bn = _blk(N, 2048)
    hb = bn // hw
    pipelined = (_nbytes((blk, K), BF16) + _nbytes((K, bn), BF16) + _nbytes((blk, V7X_LANES), BF16)
                 + _nbytes((blk, hb * (QK_PAD_DIM + V_HEAD_DIM)), BF16))
    return pl.pallas_call(
        _kvup_kernel,
        grid=(S // blk, N // bn),
        in_specs=[pl.BlockSpec((blk, K), lambda m, n: (m, 0)),
                  pl.BlockSpec((K, bn), lambda m, n: (0, n)),
                  pl.BlockSpec((blk, V7X_LANES), lambda m, n: (m, 0))],
        out_specs=[pl.BlockSpec((blk, hb * QK_PAD_DIM), lambda m, n: (m, n)),
                   pl.BlockSpec((None, hb, V_HEAD_DIM, blk), lambda m, n: (m, n, 0, 0))],
        out_shape=[jax.ShapeDtypeStruct((S, H * QK_PAD_DIM), BF16),
                   jax.ShapeDtypeStruct((S // blk, H, V_HEAD_DIM, blk), BF16)],
        compiler_params=_params(("arbitrary", "arbitrary"), pipelined, 3 * _nbytes((blk, bn), F32)),
        name="kv_up",
    )(ckvn, w_ukv, kr)


ATTN_BK = 512


def _attn_kernel(q_ref, k_ref, vt_ref, ga_ref, ct_ref, o_ref, s0_sc, s1_sc, m_sc, l_sc, acc_sc, *, bk):
    qi = pl.program_id(1)
    bq = 2 * bk
    q = q_ref[...]
    s_sc = (s0_sc, s1_sc)
    m_sc[...] = jnp.full(m_sc.shape, -jnp.inf, F32)
    l_sc[...] = jnp.zeros(l_sc.shape, F32)
    acc_sc[...] = jnp.zeros(acc_sc.shape, F32)

    def scores(j, slot):
        off = pl.multiple_of(j * bk, bk)
        s_sc[slot][...] = lax.dot_general(k_ref[pl.ds(off, bk), :], q, (((1,), (1,)), ((), ())),
                                          preferred_element_type=F32)

    def consume(j, slot, mask):
        s = s_sc[slot][...]
        if mask is not None:
            s = jnp.where(mask, s, -jnp.inf)
        m_prev = m_sc[...]
        m_new = jnp.maximum(m_prev, jnp.max(s, axis=0, keepdims=True))
        alpha = jnp.exp2(m_prev - m_new)
        p = jnp.exp2(s - m_new)
        l_sc[...] = alpha * l_sc[...] + jnp.sum(p, axis=0, keepdims=True)
        acc_sc[...] = alpha * acc_sc[...] + jnp.dot(vt_ref[j], p.astype(BF16), preferred_element_type=F32)
        m_sc[...] = m_new

    def pair(t, carry):
        scores(2 * t + 1, 1)
        consume(2 * t, 0, None)
        scores(2 * t + 2, 0)
        consume(2 * t + 1, 1, None)
        return carry

    scores(0, 0)
    lax.fori_loop(0, qi, pair, 0)

    shift = CHUNK.bit_length() - 1
    kc = lax.broadcasted_iota(jnp.int32, (bk, bq), 0) >> shift
    qc = lax.broadcasted_iota(jnp.int32, (bk, bq), 1) >> shift
    scores(2 * qi + 1, 1)
    consume(2 * qi, 0, kc <= qc)
    consume(2 * qi + 1, 1, kc + (bk >> shift) <= qc)
    attn = (acc_sc[...] / l_sc[...]).T
    o_ref[...] = (ga_ref[...].astype(F32) * attn + ct_ref[...].astype(F32)).astype(BF16)


def _attention(q, k, vt, g_attn, conv_term):
    S = q.shape[0]
    H = q.shape[1] // QK_PAD_DIM
    nblk, _, _, bk = vt.shape
    bq = 2 * bk
    assert bk % CHUNK == 0 and CHUNK & (CHUNK - 1) == 0 and S % bq == 0
    hv = pl.BlockSpec((bq, V_HEAD_DIM), lambda h, i: (i, h))
    pipelined = (_nbytes((bq, QK_PAD_DIM), BF16) + _nbytes((S, QK_PAD_DIM), BF16) + _nbytes((S, V_HEAD_DIM), BF16)
                 + 3 * _nbytes((bq, V_HEAD_DIM), BF16))
    return pl.pallas_call(
        functools.partial(_attn_kernel, bk=bk),
        grid=(H, S // bq),
        in_specs=[pl.BlockSpec((bq, QK_PAD_DIM), lambda h, i: (i, h)),
                  pl.BlockSpec((S, QK_PAD_DIM), lambda h, i: (0, h)),
                  pl.BlockSpec((nblk, None, V_HEAD_DIM, bk), lambda h, i: (0, h, 0, 0)),
                  hv, hv],
        out_specs=hv,
        out_shape=jax.ShapeDtypeStruct((S, H * V_HEAD_DIM), BF16),
        scratch_shapes=[pltpu.VMEM((bk, bq), F32), pltpu.VMEM((bk, bq), F32),
                        pltpu.VMEM((1, bq), F32), pltpu.VMEM((1, bq), F32),
                        pltpu.VMEM((V_HEAD_DIM, bq), F32)],
        compiler_params=_params(("arbitrary", "arbitrary"), pipelined, 8 * _nbytes((bk, bq), F32)),
        name="attention",
    )(q, k, vt, g_attn, conv_term)


def _mm_kernel(x_ref, w_ref, o_ref):
    kk = pl.program_id(2)

    @pl.when(kk == 0)
    def _():
        o_ref[...] = jnp.dot(x_ref[...], w_ref[...], preferred_element_type=F32)

    @pl.when(kk > 0)
    def _():
        o_ref[...] += jnp.dot(x_ref[...], w_ref[...], preferred_element_type=F32)


def _matmul(x, w):
    M, K = x.shape
    N = w.shape[-1]
    bm, bn, bk = _blk(M, 1024), _blk(N, 1024), _blk(K, 4096)
    pipelined = _nbytes((bm, bk), BF16) + _nbytes((bk, bn), BF16) + _nbytes((bm, bn), F32)
    return pl.pallas_call(
        _mm_kernel,
        grid=(M // bm, N // bn, K // bk),
        in_specs=[pl.BlockSpec((bm, bk), lambda m, n, k: (m, k)),
                  pl.BlockSpec((bk, bn), lambda m, n, k: (k, n))],
        out_specs=pl.BlockSpec((bm, bn), lambda m, n, k: (m, n)),
        out_shape=jax.ShapeDtypeStruct((M, N), F32),
        compiler_params=_params(("arbitrary", "arbitrary", "arbitrary"), pipelined, _nbytes((bm, bn), F32)),
        name="matmul",
    )(x, w)


def _ln_kernel(x_ref, f_ref, g_ref, b_ref, o32_ref, o16_ref, *, alpha):
    y = alpha * x_ref[...] + f_ref[...]
    mu = jnp.mean(y, axis=-1, keepdims=True)
    d = y - mu
    var = jnp.mean(d * d, axis=-1, keepdims=True)
    out = d * lax.rsqrt(var + LN_EPS) * g_ref[...] + b_ref[...]
    o32_ref[...] = out
    o16_ref[...] = out.astype(BF16)


def _deepnorm_ln(x, f, g, b, alpha):
    S, D = x.shape
    bm = _blk(S, 256)
    row = pl.BlockSpec((bm, D), lambda i: (i, 0))
    vec = pl.BlockSpec((1, D), lambda i: (0, 0))
    pipelined = 3 * _nbytes((bm, D), F32) + _nbytes((bm, D), BF16)
    return pl.pallas_call(
        functools.partial(_ln_kernel, alpha=alpha),
        grid=(S // bm,),
        in_specs=[row, row, vec, vec],
        out_specs=[row, row],
        out_shape=[jax.ShapeDtypeStruct((S, D), F32), jax.ShapeDtypeStruct((S, D), BF16)],
        compiler_params=_params(("arbitrary",), pipelined, 3 * _nbytes((bm, D), F32)),
        name="deepnorm_ln",
    )(x, f, g.reshape(1, -1), b.reshape(1, -1))


FFN_GROUPS = 2
FFN_BN = 512
FFN_BM = 1024


def _ffn_in_kernel(x_ref, w_ref, cw_ref, o_ref, tail_ref):
    bn = o_ref.shape[1]
    acc = jnp.dot(x_ref[...], w_ref[...], preferred_element_type=F32)
    conv = _causal_conv3(acc[:, :bn], cw_ref, tail_ref, pl.program_id(1) == 0)
    o_ref[...] = (conv * _sigmoid(conv) * acc[:, bn:]).astype(BF16)


def _ffn_in(xb, w_up, ffn_conv_w):
    S, D = xb.shape
    FF = ffn_conv_w.shape[-1]
    nt, _, wn = w_up.shape
    bn = wn // FFN_GROUPS
    assert nt * bn == FF
    bm = _blk(S, FFN_BM)
    pipelined = _nbytes((bm, D), BF16) + _nbytes((D, wn), BF16) + _nbytes((bm, bn), BF16)
    return pl.pallas_call(
        _ffn_in_kernel,
        grid=(nt, S // bm),
        in_specs=[pl.BlockSpec((bm, D), lambda n, m: (m, 0)),
                  pl.BlockSpec((None, D, wn), lambda n, m: (n, 0, 0)),
                  pl.BlockSpec((CONV_WIDTH, bn), lambda n, m: (0, n))],
        out_specs=pl.BlockSpec((bm, bn), lambda n, m: (m, n)),
        out_shape=jax.ShapeDtypeStruct((S, FF), BF16),
        scratch_shapes=[pltpu.VMEM((V7X_SUBLANES, bn), F32)],
        compiler_params=_params(("arbitrary", "arbitrary"), pipelined, 3 * _nbytes((bm, wn), F32)),
        name="ffn_in",
    )(xb, w_up, ffn_conv_w)


def kernel(x, positions, w_in, b_gate, q_norm_w, kv_norm_w, w_uq, w_ukv, conv_w, w_o, ln1_g, ln1_b,
           w_ffn_in, ffn_conv_w, w_ffn_down, ln2_g, ln2_b):
    B, S, D = x.shape
    assert B == 1, "kernels are written for a single sequence"
    depth = w_in.shape[0]
    q_lora = q_norm_w.shape[-1]
    kv_lora = kv_norm_w.shape[-1]
    n_lat = q_lora + kv_lora + QK_ROPE_DIM
    assert (q_lora + kv_lora) % V7X_LANES == 0
    H = w_uq.shape[-1] // QK_HEAD_DIM
    alpha = (2.0 * depth) ** 0.25

    w_lat = jnp.pad(w_in[:, :, :n_lat], ((0, 0), (0, 0), (0, V7X_LANES - QK_ROPE_DIM))).astype(BF16)
    w_br = _group_tiles(w_in[:, :, n_lat:].astype(BF16), BRANCH_GROUPS, _blk(D, BRANCH_BN))
    w_uq_pad = jnp.pad(w_uq.reshape(depth, q_lora, H, QK_HEAD_DIM),
                       ((0, 0), (0, 0), (0, 0), (0, QK_PAD_DIM - QK_HEAD_DIM))
                       ).reshape(depth, q_lora, H * QK_PAD_DIM).astype(BF16)
    w_ukv_b = w_ukv.astype(BF16)
    w_o_b = w_o.astype(BF16)
    w_up = _group_tiles(w_ffn_in.astype(BF16), FFN_GROUPS, _blk(ffn_conv_w.shape[-1], FFN_BN))
    w_ffn_down_b = w_ffn_down.astype(BF16)

    tabs = _rope_tables(positions.reshape(S, 1))
    xf = x.reshape(S, D)
    xb = xf.astype(BF16)
    attn_blk = _blk(S, ATTN_BK)
    for l in range(depth):
        qn, ckvn, kr = _latent_proj(xb, w_lat[l], q_norm_w[l], kv_norm_w[l], tabs)
        conv_term, g_attn = _branch_proj(xb, w_br[l], b_gate[l].reshape(1, -1), conv_w[l])
        q = _q_up(qn, w_uq_pad[l], tabs)
        k, vt = _kv_up(ckvn, w_ukv_b[l], kr, attn_blk)
        merged = _attention(q, k, vt, g_attn, conv_term)
        xf, xb = _deepnorm_ln(xf, _matmul(merged, w_o_b[l]), ln1_g[l], ln1_b[l], alpha)
        g = _ffn_in(xb, w_up[l], ffn_conv_w[l])
        xf, xb = _deepnorm_ln(xf, _matmul(g, w_ffn_down_b[l]), ln2_g[l], ln2_b[l], alpha)
    return xf.reshape(B, S, D)
```

```python
import functools
import math

import numpy as np
import jax
import jax.numpy as jnp
from jax import lax
from jax.experimental import pallas as pl
from jax.experimental.pallas import tpu as pltpu

F32 = jnp.float32
BF16 = jnp.bfloat16

CHUNK = 64
V_HEAD_DIM = 128
QK_NOPE_DIM = 128
QK_ROPE_DIM = 64
QK_HEAD_DIM = QK_NOPE_DIM + QK_ROPE_DIM
ROPE_THETA = 10000.0
CONV_WIDTH = 3
LN_EPS = 1e-5
RMS_EPS = 1e-6

V7X_LANES = 128
V7X_SUBLANES = 8
V7X_VMEM_BYTES = 64 * 1024 * 1024
V7X_VMEM_RESERVE = 6 * 1024 * 1024

QK_PAD_DIM = 2 * V7X_LANES


def _vmem_limit(pipelined_bytes, resident_bytes):
    want = 2 * pipelined_bytes + resident_bytes + V7X_VMEM_RESERVE
    return int(min(V7X_VMEM_BYTES - V7X_VMEM_RESERVE, max(want, 16 * 1024 * 1024)))


def _nbytes(shape, dtype):
    return int(np.prod(shape)) * jnp.dtype(dtype).itemsize


def _params(semantics, pipelined_bytes, resident_bytes):
    return pltpu.CompilerParams(dimension_semantics=semantics,
                                vmem_limit_bytes=_vmem_limit(pipelined_bytes, resident_bytes))


def _blk(dim, want):
    b = min(dim, want)
    assert dim % b == 0, (dim, want)
    return b


def _sigmoid(v):
    return 1.0 / (1.0 + jnp.exp(-v))


def _rope_tables_kernel(pos_ref, invf_ref, c_ref, a_ref, b_ref):
    half = QK_ROPE_DIM // 2
    ang = pos_ref[...].astype(F32) * invf_ref[...]
    lane = lax.broadcasted_iota(jnp.int32, ang.shape, 1)
    c = jnp.cos(ang)
    s = jnp.sin(ang)
    c_ref[...] = jnp.where(lane < 2 * half, c, 0.0)
    a_ref[...] = jnp.where(lane < half, -s, 0.0)
    b_ref[...] = jnp.where((lane >= half) & (lane < 2 * half), s, 0.0)


def _rope_tables(positions_col):
    S = positions_col.shape[0]
    half = QK_ROPE_DIM // 2
    inv = (np.float32(1.0) /
           (np.float32(ROPE_THETA) ** (np.arange(0, QK_ROPE_DIM, 2, dtype=np.float32) / np.float32(QK_ROPE_DIM))))
    row = np.zeros((1, V7X_LANES), np.float32)
    row[0, :half] = inv
    row[0, half:2 * half] = inv
    bs = _blk(S, 1024)
    tab = jax.ShapeDtypeStruct((S, V7X_LANES), F32)
    spec = pl.BlockSpec((bs, V7X_LANES), lambda i: (i, 0))
    return pl.pallas_call(
        _rope_tables_kernel,
        grid=(S // bs,),
        in_specs=[pl.BlockSpec((bs, 1), lambda i: (i, 0)),
                  pl.BlockSpec((1, V7X_LANES), lambda i: (0, 0))],
        out_specs=[spec, spec, spec],
        out_shape=[tab, tab, tab],
        name="rope_tables",
    )(positions_col, jnp.asarray(row))


def _rope_tile(r, c, a, b):
    return r * c + pltpu.roll(r, 96, axis=1) * a + pltpu.roll(r, 32, axis=1) * b


def _latent_kernel(x_ref, w_ref, qw_ref, kvw_ref, c_ref, a_ref, b_ref,
                   qn_ref, ckvn_ref, kr_ref, *, q_lora, kv_lora):
    acc = jnp.dot(x_ref[...], w_ref[...], preferred_element_type=F32)

    def rms(v, w):
        return v * lax.rsqrt(jnp.mean(v * v, axis=-1, keepdims=True) + RMS_EPS) * w

    qn_ref[...] = rms(acc[:, :q_lora], qw_ref[...]).astype(BF16)
    ckvn_ref[...] = rms(acc[:, q_lora:q_lora + kv_lora], kvw_ref[...]).astype(BF16)
    kr = acc[:, q_lora + kv_lora:]
    kr_ref[...] = _rope_tile(kr, c_ref[...], a_ref[...], b_ref[...]).astype(BF16)


def _latent_proj(xb, w_lat, layer, q_norm_w, kv_norm_w, tabs):
    S, D = xb.shape
    q_lora = q_norm_w.shape[-1]
    kv_lora = kv_norm_w.shape[-1]
    n_pad = w_lat.shape[-1]
    assert n_pad == q_lora + kv_lora + V7X_LANES
    bm = _blk(S, 512)
    row = lambda i: (i, 0)
    const = lambda i: (0, 0)
    tab_spec = pl.BlockSpec((bm, V7X_LANES), row)
    pipelined = (_nbytes((bm, D), BF16) + _nbytes((D, n_pad), BF16)
                 + 3 * _nbytes((bm, V7X_LANES), F32) + _nbytes((bm, n_pad), BF16))
    return pl.pallas_call(
        functools.partial(_latent_kernel, q_lora=q_lora, kv_lora=kv_lora),
        grid=(S // bm,),
        in_specs=[pl.BlockSpec((bm, D), row),
                  pl.BlockSpec((None, D, n_pad), lambda i: (layer, 0, 0)),
                  pl.BlockSpec((1, q_lora), const),
                  pl.BlockSpec((1, kv_lora), const),
                  tab_spec, tab_spec, tab_spec],
        out_specs=[pl.BlockSpec((bm, q_lora), row),
                   pl.BlockSpec((bm, kv_lora), row),
                   pl.BlockSpec((bm, V7X_LANES), row)],
        out_shape=[jax.ShapeDtypeStruct((S, q_lora), BF16),
                   jax.ShapeDtypeStruct((S, kv_lora), BF16),
                   jax.ShapeDtypeStruct((S, V7X_LANES), BF16)],
        compiler_params=_params(("arbitrary",), pipelined, 3 * _nbytes((bm, n_pad), F32)),
        name="latent_proj",
    )(xb, w_lat, q_norm_w.reshape(1, -1), kv_norm_w.reshape(1, -1), *tabs)


ROW_SPLIT = 2


def _reset_conv_tail(tail_ref, first_tile):
    @pl.when(first_tile)
    def _():
        tail_ref[...] = jnp.zeros_like(tail_ref)


def _causal_conv3(z, w_ref, tail_ref):
    bm, bn = z.shape
    prev1 = tail_ref[V7X_SUBLANES - 1:V7X_SUBLANES, :]
    prev2 = tail_ref[V7X_SUBLANES - 2:V7X_SUBLANES - 1, :]
    row = lax.broadcasted_iota(jnp.int32, (bm, bn), 0)
    z1 = jnp.where(row == 0, prev1, pltpu.roll(z, 1, axis=0))
    z2 = pltpu.roll(z, 2, axis=0)
    z2 = jnp.where(row == 0, prev2, jnp.where(row == 1, prev1, z2))
    tail_ref[...] = z[bm - V7X_SUBLANES:, :]
    return z2 * w_ref[0:1, :] + z1 * w_ref[1:2, :] + z * w_ref[2:3, :]


BRANCH_GROUPS = 5
BRANCH_BN = 256
BRANCH_BM = 1024
REGROUP_BD = 1024


def _regroup_kernel(*refs, groups, bn, shift):
    o_ref = refs[-1]
    nb = bn // V7X_LANES + 1
    lane = lax.broadcasted_iota(jnp.int32, refs[0].shape, 1)
    for g in range(groups):
        tiles = [refs[g * nb + i][...] for i in range(nb)]
        if shift:
            tiles = [pltpu.roll(t, V7X_LANES - shift, axis=1) for t in tiles]
        for i in range(nb - 1):
            out = jnp.where(lane < V7X_LANES - shift, tiles[i], tiles[i + 1]) if shift else tiles[i]
            o_ref[:, g * bn + i * V7X_LANES:g * bn + (i + 1) * V7X_LANES] = out.astype(BF16)


def _regroup_cast(w, col0, groups, C, bn):
    depth, D, _ = w.shape
    nt = C // bn
    shift = col0 % V7X_LANES
    nb = bn // V7X_LANES + 1
    bd = _blk(D, REGROUP_BD)

    def in_spec(g, i):
        base = (col0 - shift + g * C) // V7X_LANES + i
        return pl.BlockSpec((None, bd, V7X_LANES), lambda l, n, d: (l, d, base + n * (bn // V7X_LANES)))

    pipelined = groups * nb * _nbytes((bd, V7X_LANES), F32) + _nbytes((bd, groups * bn), BF16)
    return pl.pallas_call(
        functools.partial(_regroup_kernel, groups=groups, bn=bn, shift=shift),
        grid=(depth, nt, D // bd),
        in_specs=[in_spec(g, i) for g in range(groups) for i in range(nb)],
        out_specs=pl.BlockSpec((None, None, bd, groups * bn), lambda l, n, d: (l, n, d, 0)),
        out_shape=jax.ShapeDtypeStruct((depth, nt, D, groups * bn), BF16),
        compiler_params=_params(("arbitrary", "arbitrary", "arbitrary"), pipelined,
                                _nbytes((bd, groups * bn), F32)),
        name="regroup_cast",
    )(*([w] * (groups * nb)))


def _branch_kernel(x_ref, w_ref, ba_ref, bc_ref, cw_ref, ct_ref, ga_ref, tail_ref):
    bn = ct_ref.shape[1]
    _reset_conv_tail(tail_ref, pl.program_id(1) == 0)
    hm = x_ref.shape[0] // ROW_SPLIT
    for r in range(ROW_SPLIT):
        rows = slice(r * hm, (r + 1) * hm)
        acc = jnp.dot(x_ref[rows, :], w_ref[...], preferred_element_type=F32)
        cb, cc, ch, ga, gc = (acc[:, g * bn:(g + 1) * bn] for g in range(BRANCH_GROUPS))
        conv = _causal_conv3(cc * ch, cw_ref, tail_ref)
        ct_ref[rows, :] = (_sigmoid(gc + bc_ref[...]) * cb * conv).astype(BF16)
        ga_ref[rows, :] = _sigmoid(ga + ba_ref[...]).astype(BF16)


def _branch_proj(xb, w_br, layer, b_gate, conv_w):
    S, D = xb.shape
    C = conv_w.shape[-1]
    _, nt, _, wn = w_br.shape
    bn = wn // BRANCH_GROUPS
    assert nt * bn == C and b_gate.shape == (1, 2 * C)
    bm = _blk(S, BRANCH_BM)
    out_spec = pl.BlockSpec((bm, bn), lambda n, m: (m, n))
    pipelined = (_nbytes((bm, D), BF16) + _nbytes((D, wn), BF16) + 2 * _nbytes((bm, bn), BF16))
    return pl.pallas_call(
        _branch_kernel,
        grid=(nt, S // bm),
        in_specs=[pl.BlockSpec((bm, D), lambda n, m: (m, 0)),
                  pl.BlockSpec((None, None, D, wn), lambda n, m: (layer, n, 0, 0)),
                  pl.BlockSpec((1, bn), lambda n, m: (0, n)),
                  pl.BlockSpec((1, bn), lambda n, m: (0, nt + n)),
                  pl.BlockSpec((CONV_WIDTH, bn), lambda n, m: (0, n))],
        out_specs=[out_spec, out_spec],
        out_shape=[jax.ShapeDtypeStruct((S, C), BF16), jax.ShapeDtypeStruct((S, C), BF16)],
        scratch_shapes=[pltpu.VMEM((V7X_SUBLANES, bn), F32)],
        compiler_params=_params(("arbitrary", "arbitrary"), pipelined, 3 * _nbytes((bm, wn), F32)),
        name="branch_proj",
    )(xb, w_br, b_gate, b_gate, conv_w)


def _qup_kernel(x_ref, w_ref, c_ref, a_ref, b_ref, o_ref, *, scale):
    acc = jnp.dot(x_ref[...], w_ref[...], preferred_element_type=F32)
    c, a, b = c_ref[...], a_ref[...], b_ref[...]
    for h in range(acc.shape[1] // QK_PAD_DIM):
        lo = h * QK_PAD_DIM
        mid = lo + QK_NOPE_DIM
        o_ref[:, lo:mid] = (acc[:, lo:mid] * scale).astype(BF16)
        o_ref[:, mid:lo + QK_PAD_DIM] = (_rope_tile(acc[:, mid:lo + QK_PAD_DIM], c, a, b) * scale).astype(BF16)


def _q_up(qn, w_uq_pad, layer, tabs):
    S, K = qn.shape
    N = w_uq_pad.shape[-1]
    bm = _blk(S, 1024)
    bn = _blk(N, 1024)
    tab_spec = pl.BlockSpec((bm, V7X_LANES), lambda m, n: (m, 0))
    pipelined = (_nbytes((bm, K), BF16) + _nbytes((K, bn), BF16) + 3 * _nbytes((bm, V7X_LANES), F32)
                 + _nbytes((bm, bn), BF16))
    return pl.pallas_call(
        functools.partial(_qup_kernel, scale=math.log2(math.e) / math.sqrt(QK_HEAD_DIM)),
        grid=(S // bm, N // bn),
        in_specs=[pl.BlockSpec((bm, K), lambda m, n: (m, 0)),
                  pl.BlockSpec((None, K, bn), lambda m, n: (layer, 0, n)),
                  tab_spec, tab_spec, tab_spec],
        out_specs=pl.BlockSpec((bm, bn), lambda m, n: (m, n)),
        out_shape=jax.ShapeDtypeStruct((S, N), BF16),
        compiler_params=_params(("arbitrary", "arbitrary"), pipelined, 2 * _nbytes((bm, bn), F32)),
        name="q_up",
    )(qn, w_uq_pad, *tabs)


BF16_SUBLANE_ROWS = 2 * V7X_SUBLANES
VT_ROWS = V_HEAD_DIM + BF16_SUBLANE_ROWS


def _kvup_kernel(x_ref, w_ref, kr_ref, k_ref, vt_ref):
    acc = jnp.dot(x_ref[...], w_ref[...], preferred_element_type=F32)
    kr = kr_ref[...]
    hw = QK_NOPE_DIM + V_HEAD_DIM
    ones = jnp.ones((BF16_SUBLANE_ROWS, acc.shape[0]), BF16)
    for h in range(acc.shape[1] // hw):
        k_ref[:, h * QK_PAD_DIM:h * QK_PAD_DIM + QK_NOPE_DIM] = acc[:, h * hw:h * hw + QK_NOPE_DIM].astype(BF16)
        k_ref[:, h * QK_PAD_DIM + QK_NOPE_DIM:(h + 1) * QK_PAD_DIM] = kr
        vt_ref[h, :V_HEAD_DIM, :] = acc[:, h * hw + QK_NOPE_DIM:(h + 1) * hw].T.astype(BF16)
        vt_ref[h, V_HEAD_DIM:, :] = ones


def _kv_up(ckvn, w_ukv, layer, kr, blk):
    S, K = ckvn.shape
    N = w_ukv.shape[-1]
    hw = QK_NOPE_DIM + V_HEAD_DIM
    H = N // hw
    bn = _blk(N, 2048)
    hb = bn // hw
    pipelined = (_nbytes((blk, K), BF16) + _nbytes((K, bn), BF16) + _nbytes((blk, V7X_LANES), BF16)
                 + _nbytes((blk, hb * (QK_PAD_DIM + VT_ROWS)), BF16))
    return pl.pallas_call(
        _kvup_kernel,
        grid=(S // blk, N // bn),
        in_specs=[pl.BlockSpec((blk, K), lambda m, n: (m, 0)),
                  pl.BlockSpec((None, K, bn), lambda m, n: (layer, 0, n)),
                  pl.BlockSpec((blk, V7X_LANES), lambda m, n: (m, 0))],
        out_specs=[pl.BlockSpec((blk, hb * QK_PAD_DIM), lambda m, n: (m, n)),
                   pl.BlockSpec((None, hb, VT_ROWS, blk), lambda m, n: (m, n, 0, 0))],
        out_shape=[jax.ShapeDtypeStruct((S, H * QK_PAD_DIM), BF16),
                   jax.ShapeDtypeStruct((S // blk, H, VT_ROWS, blk), BF16)],
        compiler_params=_params(("arbitrary", "arbitrary"), pipelined, 3 * _nbytes((blk, bn), F32)),
        name="kv_up",
    )(ckvn, w_ukv, kr)


ATTN_BK = 512


def _attn_kernel(q_ref, k_ref, vt_ref, ga_ref, ct_ref, o_ref, s0_sc, s1_sc, m_sc, acc_sc, *, bk):
    qi = pl.program_id(1)
    bq = 2 * bk
    s_sc = (s0_sc, s1_sc)
    m_sc[...] = jnp.full(m_sc.shape, -jnp.inf, F32)
    acc_sc[...] = jnp.zeros(acc_sc.shape, F32)

    def scores(j, slot):
        off = pl.multiple_of(j * bk, bk)
        s_sc[slot][...] = lax.dot_general(k_ref[pl.ds(off, bk), :], q_ref[...], (((1,), (1,)), ((), ())),
                                          preferred_element_type=F32)

    def consume(j, slot, mask):
        s = s_sc[slot][...]
        if mask is not None:
            s = jnp.where(mask, s, -jnp.inf)
        m_prev = m_sc[...]
        m_new = jnp.maximum(m_prev, jnp.max(s, axis=0, keepdims=True))
        alpha = jnp.exp2(m_prev - m_new)
        p = jnp.exp2(s - m_new).astype(BF16)
        acc_sc[...] = alpha * acc_sc[...] + jnp.dot(vt_ref[j], p, preferred_element_type=F32)
        m_sc[...] = m_new

    def pair(t, carry):
        scores(2 * t + 1, 1)
        consume(2 * t, 0, None)
        scores(2 * t + 2, 0)
        consume(2 * t + 1, 1, None)
        return carry

    scores(0, 0)
    lax.fori_loop(0, qi, pair, 0)

    shift = CHUNK.bit_length() - 1
    kc = lax.broadcasted_iota(jnp.int32, (bk, bq), 0) >> shift
    qc = lax.broadcasted_iota(jnp.int32, (bk, bq), 1) >> shift
    scores(2 * qi + 1, 1)
    consume(2 * qi, 0, kc <= qc)
    consume(2 * qi + 1, 1, kc + (bk >> shift) <= qc)
    attn = (acc_sc[:V_HEAD_DIM, :] / acc_sc[V_HEAD_DIM:V_HEAD_DIM + 1, :]).T
    o_ref[...] = (ga_ref[...].astype(F32) * attn + ct_ref[...].astype(F32)).astype(BF16)


def _attention(q, k, vt, g_attn, conv_term):
    S = q.shape[0]
    H = q.shape[1] // QK_PAD_DIM
    nblk, _, vrows, bk = vt.shape
    bq = 2 * bk
    assert vrows == VT_ROWS and bk % CHUNK == 0 and CHUNK & (CHUNK - 1) == 0 and S % bq == 0
    hv = pl.BlockSpec((bq, V_HEAD_DIM), lambda h, i: (i, h))
    pipelined = (_nbytes((bq, QK_PAD_DIM), BF16) + _nbytes((S, QK_PAD_DIM), BF16) + _nbytes((S, VT_ROWS), BF16)
                 + 3 * _nbytes((bq, V_HEAD_DIM), BF16))
    return pl.pallas_call(
        functools.partial(_attn_kernel, bk=bk),
        grid=(H, S // bq),
        in_specs=[pl.BlockSpec((bq, QK_PAD_DIM), lambda h, i: (i, h)),
                  pl.BlockSpec((S, QK_PAD_DIM), lambda h, i: (0, h)),
                  pl.BlockSpec((nblk, None, VT_ROWS, bk), lambda h, i: (0, h, 0, 0)),
                  hv, hv],
        out_specs=hv,
        out_shape=jax.ShapeDtypeStruct((S, H * V_HEAD_DIM), BF16),
        scratch_shapes=[pltpu.VMEM((bk, bq), F32), pltpu.VMEM((bk, bq), F32),
                        pltpu.VMEM((1, bq), F32),
                        pltpu.VMEM((VT_ROWS, bq), F32)],
        compiler_params=_params(("arbitrary", "arbitrary"), pipelined, 8 * _nbytes((bk, bq), F32)),
        name="attention",
    )(q, k, vt, g_attn, conv_term)


def _mm_kernel(x_ref, w_ref, o_ref):
    kk = pl.program_id(2)

    @pl.when(kk == 0)
    def _():
        o_ref[...] = jnp.dot(x_ref[...], w_ref[...], preferred_element_type=F32)

    @pl.when(kk > 0)
    def _():
        o_ref[...] += jnp.dot(x_ref[...], w_ref[...], preferred_element_type=F32)


def _matmul(x, w, layer):
    M, K = x.shape
    N = w.shape[-1]
    bm, bn, bk = _blk(M, 1024), _blk(N, 1024), _blk(K, 4096)
    pipelined = _nbytes((bm, bk), BF16) + _nbytes((bk, bn), BF16) + _nbytes((bm, bn), F32)
    return pl.pallas_call(
        _mm_kernel,
        grid=(M // bm, N // bn, K // bk),
        in_specs=[pl.BlockSpec((bm, bk), lambda m, n, k: (m, k)),
                  pl.BlockSpec((None, bk, bn), lambda m, n, k: (layer, k, n))],
        out_specs=pl.BlockSpec((bm, bn), lambda m, n, k: (m, n)),
        out_shape=jax.ShapeDtypeStruct((M, N), F32),
        compiler_params=_params(("arbitrary", "arbitrary", "arbitrary"), pipelined, _nbytes((bm, bn), F32)),
        name="matmul",
    )(x, w)


def _ln_kernel(x_ref, f_ref, g_ref, b_ref, o32_ref, o16_ref, *, alpha):
    y = alpha * x_ref[...] + f_ref[...]
    mu = jnp.mean(y, axis=-1, keepdims=True)
    d = y - mu
    var = jnp.mean(d * d, axis=-1, keepdims=True)
    out = d * lax.rsqrt(var + LN_EPS) * g_ref[...] + b_ref[...]
    o32_ref[...] = out
    o16_ref[...] = out.astype(BF16)


def _deepnorm_ln(x, f, g, b, alpha):
    S, D = x.shape
    bm = _blk(S, 256)
    row = pl.BlockSpec((bm, D), lambda i: (i, 0))
    vec = pl.BlockSpec((1, D), lambda i: (0, 0))
    pipelined = 3 * _nbytes((bm, D), F32) + _nbytes((bm, D), BF16)
    return pl.pallas_call(
        functools.partial(_ln_kernel, alpha=alpha),
        grid=(S // bm,),
        in_specs=[row, row, vec, vec],
        out_specs=[row, row],
        out_shape=[jax.ShapeDtypeStruct((S, D), F32), jax.ShapeDtypeStruct((S, D), BF16)],
        compiler_params=_params(("arbitrary",), pipelined, 3 * _nbytes((bm, D), F32)),
        name="deepnorm_ln",
    )(x, f, g.reshape(1, -1), b.reshape(1, -1))


FFN_BN = 512
FFN_BM = 1024


def _ffn_in_kernel(x_ref, wa_ref, wu_ref, cw_ref, o_ref, tail_ref):
    _reset_conv_tail(tail_ref, pl.program_id(1) == 0)
    hm = x_ref.shape[0] // ROW_SPLIT
    for r in range(ROW_SPLIT):
        rows = slice(r * hm, (r + 1) * hm)
        x = x_ref[rows, :]
        a = jnp.dot(x, wa_ref[...], preferred_element_type=F32)
        conv = _causal_conv3(a, cw_ref, tail_ref)
        u = jnp.dot(x, wu_ref[...], preferred_element_type=F32)
        o_ref[rows, :] = (conv * _sigmoid(conv) * u).astype(BF16)


def _ffn_in(xb, w_ffn_in, layer, ffn_conv_w):
    S, D = xb.shape
    FF = ffn_conv_w.shape[-1]
    assert w_ffn_in.shape[1:] == (D, 2 * FF)
    bm = _blk(S, FFN_BM)
    bn = _blk(FF, FFN_BN)
    nt = FF // bn
    pipelined = _nbytes((bm, D), BF16) + 2 * _nbytes((D, bn), BF16) + _nbytes((bm, bn), BF16)
    return pl.pallas_call(
        _ffn_in_kernel,
        grid=(nt, S // bm),
        in_specs=[pl.BlockSpec((bm, D), lambda n, m: (m, 0)),
                  pl.BlockSpec((None, D, bn), lambda n, m: (layer, 0, n)),
                  pl.BlockSpec((None, D, bn), lambda n, m: (layer, 0, nt + n)),
                  pl.BlockSpec((CONV_WIDTH, bn), lambda n, m: (0, n))],
        out_specs=pl.BlockSpec((bm, bn), lambda n, m: (m, n)),
        out_shape=jax.ShapeDtypeStruct((S, FF), BF16),
        scratch_shapes=[pltpu.VMEM((V7X_SUBLANES, bn), F32)],
        compiler_params=_params(("arbitrary", "arbitrary"), pipelined, 6 * _nbytes((bm, bn), F32)),
        name="ffn_in",
    )(xb, w_ffn_in, w_ffn_in, ffn_conv_w)


def kernel(x, positions, w_in, b_gate, q_norm_w, kv_norm_w, w_uq, w_ukv, conv_w, w_o, ln1_g, ln1_b,
           w_ffn_in, ffn_conv_w, w_ffn_down, ln2_g, ln2_b):
    B, S, D = x.shape
    assert B == 1, "kernels are written for a single sequence"
    depth = w_in.shape[0]
    q_lora = q_norm_w.shape[-1]
    kv_lora = kv_norm_w.shape[-1]
    n_lat = q_lora + kv_lora + QK_ROPE_DIM
    assert (q_lora + kv_lora) % V7X_LANES == 0
    H = w_uq.shape[-1] // QK_HEAD_DIM
    alpha = (2.0 * depth) ** 0.25

    w_lat = jnp.pad(w_in[:, :, :n_lat], ((0, 0), (0, 0), (0, V7X_LANES - QK_ROPE_DIM))).astype(BF16)
    w_br = _regroup_cast(w_in, n_lat, BRANCH_GROUPS, D, _blk(D, BRANCH_BN))
    w_uq_pad = jnp.pad(w_uq.reshape(depth, q_lora, H, QK_HEAD_DIM),
                       ((0, 0), (0, 0), (0, 0), (0, QK_PAD_DIM - QK_HEAD_DIM))
                       ).reshape(depth, q_lora, H * QK_PAD_DIM).astype(BF16)
    w_ukv_b = w_ukv.astype(BF16)
    w_o_b = w_o.astype(BF16)
    w_ffn_in_b = w_ffn_in.astype(BF16)
    w_ffn_down_b = w_ffn_down.astype(BF16)

    tabs = _rope_tables(positions.reshape(S, 1))
    xf = x.reshape(S, D)
    xb = xf.astype(BF16)
    attn_blk = _blk(S, ATTN_BK)
    for l in range(depth):
        qn, ckvn, kr = _latent_proj(xb, w_lat, l, q_norm_w[l], kv_norm_w[l], tabs)
        conv_term, g_attn = _branch_proj(xb, w_br, l, b_gate[l].reshape(1, -1), conv_w[l])
        q = _q_up(qn, w_uq_pad, l, tabs)
        k, vt = _kv_up(ckvn, w_ukv_b, l, kr, attn_blk)
        merged = _attention(q, k, vt, g_attn, conv_term)
        xf, xb = _deepnorm_ln(xf, _matmul(merged, w_o_b, l), ln1_g[l], ln1_b[l], alpha)
        g = _ffn_in(xb, w_ffn_in_b, l, ffn_conv_w[l])
        xf, xb = _deepnorm_ln(xf, _matmul(g, w_ffn_down_b, l), ln2_g[l], ln2_b[l], alpha)
    return xf.reshape(B, S, D)
```

```python
import functools
import math

import numpy as np
import jax
import jax.numpy as jnp
from jax import lax
from jax.experimental import pallas as pl
from jax.experimental.pallas import tpu as pltpu

F32 = jnp.float32
BF16 = jnp.bfloat16

CHUNK = 64
V_HEAD_DIM = 128
QK_NOPE_DIM = 128
QK_ROPE_DIM = 64
QK_HEAD_DIM = QK_NOPE_DIM + QK_ROPE_DIM
ROPE_THETA = 10000.0
CONV_WIDTH = 3
LN_EPS = 1e-5
RMS_EPS = 1e-6

V7X_LANES = 128
V7X_SUBLANES = 8
V7X_VMEM_BYTES = 64 * 1024 * 1024
V7X_VMEM_RESERVE = 6 * 1024 * 1024

QK_PAD_DIM = 2 * V7X_LANES


def _vmem_limit(pipelined_bytes, resident_bytes):
    want = 2 * pipelined_bytes + resident_bytes + V7X_VMEM_RESERVE
    return int(min(V7X_VMEM_BYTES - V7X_VMEM_RESERVE, max(want, 16 * 1024 * 1024)))


def _nbytes(shape, dtype):
    return int(np.prod(shape)) * jnp.dtype(dtype).itemsize


def _params(semantics, pipelined_bytes, resident_bytes):
    return pltpu.CompilerParams(dimension_semantics=semantics,
                                vmem_limit_bytes=_vmem_limit(pipelined_bytes, resident_bytes))


def _blk(dim, want):
    b = min(dim, want)
    assert dim % b == 0, (dim, want)
    return b


def _sigmoid(v):
    return 1.0 / (1.0 + jnp.exp(-v))


def _rope_tables_kernel(pos_ref, invf_ref, c_ref, a_ref, b_ref):
    half = QK_ROPE_DIM // 2
    ang = pos_ref[...].astype(F32) * invf_ref[...]
    lane = lax.broadcasted_iota(jnp.int32, ang.shape, 1)
    c = jnp.cos(ang)
    s = jnp.sin(ang)
    c_ref[...] = jnp.where(lane < 2 * half, c, 0.0)
    a_ref[...] = jnp.where(lane < half, -s, 0.0)
    b_ref[...] = jnp.where((lane >= half) & (lane < 2 * half), s, 0.0)


def _rope_tables(positions_col):
    S = positions_col.shape[0]
    half = QK_ROPE_DIM // 2
    inv = (np.float32(1.0) /
           (np.float32(ROPE_THETA) ** (np.arange(0, QK_ROPE_DIM, 2, dtype=np.float32) / np.float32(QK_ROPE_DIM))))
    row = np.zeros((1, V7X_LANES), np.float32)
    row[0, :half] = inv
    row[0, half:2 * half] = inv
    bs = _blk(S, 1024)
    tab = jax.ShapeDtypeStruct((S, V7X_LANES), F32)
    spec = pl.BlockSpec((bs, V7X_LANES), lambda i: (i, 0))
    return pl.pallas_call(
        _rope_tables_kernel,
        grid=(S // bs,),
        in_specs=[pl.BlockSpec((bs, 1), lambda i: (i, 0)),
                  pl.BlockSpec((1, V7X_LANES), lambda i: (0, 0))],
        out_specs=[spec, spec, spec],
        out_shape=[tab, tab, tab],
        name="rope_tables",
    )(positions_col, jnp.asarray(row))


def _rope_tile(r, c, a, b):
    return r * c + pltpu.roll(r, 96, axis=1) * a + pltpu.roll(r, 32, axis=1) * b


def _latent_kernel(x_ref, w_ref, qw_ref, kvw_ref, c_ref, a_ref, b_ref,
                   qn_ref, ckvn_ref, kr_ref, *, q_lora, kv_lora):
    acc = jnp.dot(x_ref[...], w_ref[...], preferred_element_type=F32)

    def rms(v, w):
        return v * lax.rsqrt(jnp.mean(v * v, axis=-1, keepdims=True) + RMS_EPS) * w

    qn_ref[...] = rms(acc[:, :q_lora], qw_ref[...]).astype(BF16)
    ckvn_ref[...] = rms(acc[:, q_lora:q_lora + kv_lora], kvw_ref[...]).astype(BF16)
    kr = acc[:, q_lora + kv_lora:]
    kr_ref[...] = _rope_tile(kr, c_ref[...], a_ref[...], b_ref[...]).astype(BF16)


def _latent_proj(xb, w_lat, layer, q_norm_w, kv_norm_w, tabs):
    S, D = xb.shape
    q_lora = q_norm_w.shape[-1]
    kv_lora = kv_norm_w.shape[-1]
    n_pad = w_lat.shape[-1]
    assert n_pad == q_lora + kv_lora + V7X_LANES
    bm = _blk(S, 512)
    row = lambda i: (i, 0)
    const = lambda i: (0, 0)
    tab_spec = pl.BlockSpec((bm, V7X_LANES), row)
    pipelined = (_nbytes((bm, D), BF16) + _nbytes((D, n_pad), BF16)
                 + 3 * _nbytes((bm, V7X_LANES), F32) + _nbytes((bm, n_pad), BF16))
    return pl.pallas_call(
        functools.partial(_latent_kernel, q_lora=q_lora, kv_lora=kv_lora),
        grid=(S // bm,),
        in_specs=[pl.BlockSpec((bm, D), row),
                  pl.BlockSpec((None, D, n_pad), lambda i: (layer, 0, 0)),
                  pl.BlockSpec((1, q_lora), const),
                  pl.BlockSpec((1, kv_lora), const),
                  tab_spec, tab_spec, tab_spec],
        out_specs=[pl.BlockSpec((bm, q_lora), row),
                   pl.BlockSpec((bm, kv_lora), row),
                   pl.BlockSpec((bm, V7X_LANES), row)],
        out_shape=[jax.ShapeDtypeStruct((S, q_lora), BF16),
                   jax.ShapeDtypeStruct((S, kv_lora), BF16),
                   jax.ShapeDtypeStruct((S, V7X_LANES), BF16)],
        compiler_params=_params(("arbitrary",), pipelined, 3 * _nbytes((bm, n_pad), F32)),
        name="latent_proj",
    )(xb, w_lat, q_norm_w.reshape(1, -1), kv_norm_w.reshape(1, -1), *tabs)


ROW_SPLIT = 2


def _reset_conv_tail(tail_ref, first_tile):
    @pl.when(first_tile)
    def _():
        tail_ref[...] = jnp.zeros_like(tail_ref)


def _causal_conv3(z, w_ref, tail_ref):
    bm, bn = z.shape
    prev1 = tail_ref[V7X_SUBLANES - 1:V7X_SUBLANES, :]
    prev2 = tail_ref[V7X_SUBLANES - 2:V7X_SUBLANES - 1, :]
    row = lax.broadcasted_iota(jnp.int32, (bm, bn), 0)
    z1 = jnp.where(row == 0, prev1, pltpu.roll(z, 1, axis=0))
    z2 = pltpu.roll(z, 2, axis=0)
    z2 = jnp.where(row == 0, prev2, jnp.where(row == 1, prev1, z2))
    tail_ref[...] = z[bm - V7X_SUBLANES:, :]
    return z2 * w_ref[0:1, :] + z1 * w_ref[1:2, :] + z * w_ref[2:3, :]


BRANCH_GROUPS = 5
BRANCH_BN = 256
BRANCH_BM = 1024
REGROUP_BD = 1024


def _regroup_kernel(*refs, groups, bn):
    o_ref = refs[-1]
    for g in range(groups):
        o_ref[:, g * bn:(g + 1) * bn] = refs[g][0].T.astype(BF16)


def _regroup_cast(w_t, col0, groups, C, bn):
    depth, _, D = w_t.shape
    nt = C // bn
    bd = _blk(D, REGROUP_BD)
    assert col0 % V7X_SUBLANES == 0

    def in_spec(g):
        return pl.BlockSpec((pl.Element(1), pl.Element(bn), pl.Element(bd)),
                            lambda l, n, d: (l, pl.multiple_of(col0 + g * C + n * bn, V7X_SUBLANES),
                                             pl.multiple_of(d * bd, V7X_LANES)))

    pipelined = groups * _nbytes((bn, bd), F32) + _nbytes((bd, groups * bn), BF16)
    return pl.pallas_call(
        functools.partial(_regroup_kernel, groups=groups, bn=bn),
        grid=(depth, nt, D // bd),
        in_specs=[in_spec(g) for g in range(groups)],
        out_specs=pl.BlockSpec((None, None, bd, groups * bn), lambda l, n, d: (l, n, d, 0)),
        out_shape=jax.ShapeDtypeStruct((depth, nt, D, groups * bn), BF16),
        compiler_params=_params(("arbitrary", "arbitrary", "arbitrary"), pipelined,
                                2 * _nbytes((bd, groups * bn), F32)),
        name="regroup_cast",
    )(*([w_t] * groups))


def _branch_kernel(x_ref, w_ref, ba_ref, bc_ref, cw_ref, ct_ref, ga_ref, tail_ref):
    bn = ct_ref.shape[1]
    _reset_conv_tail(tail_ref, pl.program_id(1) == 0)
    hm = x_ref.shape[0] // ROW_SPLIT
    for r in range(ROW_SPLIT):
        rows = slice(r * hm, (r + 1) * hm)
        acc = jnp.dot(x_ref[rows, :], w_ref[...], preferred_element_type=F32)
        cb, cc, ch, ga, gc = (acc[:, g * bn:(g + 1) * bn] for g in range(BRANCH_GROUPS))
        conv = _causal_conv3(cc * ch, cw_ref, tail_ref)
        ct_ref[rows, :] = (_sigmoid(gc + bc_ref[...]) * cb * conv).astype(BF16)
        ga_ref[rows, :] = _sigmoid(ga + ba_ref[...]).astype(BF16)


def _branch_proj(xb, w_br, layer, b_gate, conv_w):
    S, D = xb.shape
    C = conv_w.shape[-1]
    _, nt, _, wn = w_br.shape
    bn = wn // BRANCH_GROUPS
    assert nt * bn == C and b_gate.shape == (1, 2 * C)
    bm = _blk(S, BRANCH_BM)
    out_spec = pl.BlockSpec((bm, bn), lambda n, m: (m, n))
    pipelined = (_nbytes((bm, D), BF16) + _nbytes((D, wn), BF16) + 2 * _nbytes((bm, bn), BF16))
    return pl.pallas_call(
        _branch_kernel,
        grid=(nt, S // bm),
        in_specs=[pl.BlockSpec((bm, D), lambda n, m: (m, 0)),
                  pl.BlockSpec((None, None, D, wn), lambda n, m: (layer, n, 0, 0)),
                  pl.BlockSpec((1, bn), lambda n, m: (0, n)),
                  pl.BlockSpec((1, bn), lambda n, m: (0, nt + n)),
                  pl.BlockSpec((CONV_WIDTH, bn), lambda n, m: (0, n))],
        out_specs=[out_spec, out_spec],
        out_shape=[jax.ShapeDtypeStruct((S, C), BF16), jax.ShapeDtypeStruct((S, C), BF16)],
        scratch_shapes=[pltpu.VMEM((V7X_SUBLANES, bn), F32)],
        compiler_params=_params(("arbitrary", "arbitrary"), pipelined, 3 * _nbytes((bm, wn), F32)),
        name="branch_proj",
    )(xb, w_br, b_gate, b_gate, conv_w)


def _qup_kernel(x_ref, w_ref, c_ref, a_ref, b_ref, o_ref, *, scale):
    acc = jnp.dot(x_ref[...], w_ref[...], preferred_element_type=F32)
    c, a, b = c_ref[...], a_ref[...], b_ref[...]
    for h in range(acc.shape[1] // QK_PAD_DIM):
        lo = h * QK_PAD_DIM
        mid = lo + QK_NOPE_DIM
        o_ref[:, lo:mid] = (acc[:, lo:mid] * scale).astype(BF16)
        o_ref[:, mid:lo + QK_PAD_DIM] = (_rope_tile(acc[:, mid:lo + QK_PAD_DIM], c, a, b) * scale).astype(BF16)


def _q_up(qn, w_uq_pad, layer, tabs):
    S, K = qn.shape
    N = w_uq_pad.shape[-1]
    bm = _blk(S, 1024)
    bn = _blk(N, 1024)
    tab_spec = pl.BlockSpec((bm, V7X_LANES), lambda m, n: (m, 0))
    pipelined = (_nbytes((bm, K), BF16) + _nbytes((K, bn), BF16) + 3 * _nbytes((bm, V7X_LANES), F32)
                 + _nbytes((bm, bn), BF16))
    return pl.pallas_call(
        functools.partial(_qup_kernel, scale=math.log2(math.e) / math.sqrt(QK_HEAD_DIM)),
        grid=(S // bm, N // bn),
        in_specs=[pl.BlockSpec((bm, K), lambda m, n: (m, 0)),
                  pl.BlockSpec((None, K, bn), lambda m, n: (layer, 0, n)),
                  tab_spec, tab_spec, tab_spec],
        out_specs=pl.BlockSpec((bm, bn), lambda m, n: (m, n)),
        out_shape=jax.ShapeDtypeStruct((S, N), BF16),
        compiler_params=_params(("arbitrary", "arbitrary"), pipelined, 2 * _nbytes((bm, bn), F32)),
        name="q_up",
    )(qn, w_uq_pad, *tabs)


BF16_SUBLANE_ROWS = 2 * V7X_SUBLANES
VT_ROWS = V_HEAD_DIM + BF16_SUBLANE_ROWS


def _kvup_kernel(x_ref, w_ref, kr_ref, k_ref, vt_ref):
    acc = jnp.dot(x_ref[...], w_ref[...], preferred_element_type=F32)
    kr = kr_ref[...]
    hw = QK_NOPE_DIM + V_HEAD_DIM
    ones = jnp.ones((BF16_SUBLANE_ROWS, acc.shape[0]), BF16)
    for h in range(acc.shape[1] // hw):
        k_ref[:, h * QK_PAD_DIM:h * QK_PAD_DIM + QK_NOPE_DIM] = acc[:, h * hw:h * hw + QK_NOPE_DIM].astype(BF16)
        k_ref[:, h * QK_PAD_DIM + QK_NOPE_DIM:(h + 1) * QK_PAD_DIM] = kr
        vt_ref[h, :V_HEAD_DIM, :] = acc[:, h * hw + QK_NOPE_DIM:(h + 1) * hw].T.astype(BF16)
        vt_ref[h, V_HEAD_DIM:, :] = ones


def _kv_up(ckvn, w_ukv, layer, kr, blk):
    S, K = ckvn.shape
    N = w_ukv.shape[-1]
    hw = QK_NOPE_DIM + V_HEAD_DIM
    H = N // hw
    bn = _blk(N, 2048)
    hb = bn // hw
    pipelined = (_nbytes((blk, K), BF16) + _nbytes((K, bn), BF16) + _nbytes((blk, V7X_LANES), BF16)
                 + _nbytes((blk, hb * (QK_PAD_DIM + VT_ROWS)), BF16))
    return pl.pallas_call(
        _kvup_kernel,
        grid=(S // blk, N // bn),
        in_specs=[pl.BlockSpec((blk, K), lambda m, n: (m, 0)),
                  pl.BlockSpec((None, K, bn), lambda m, n: (layer, 0, n)),
                  pl.BlockSpec((blk, V7X_LANES), lambda m, n: (m, 0))],
        out_specs=[pl.BlockSpec((blk, hb * QK_PAD_DIM), lambda m, n: (m, n)),
                   pl.BlockSpec((None, hb, VT_ROWS, blk), lambda m, n: (m, n, 0, 0))],
        out_shape=[jax.ShapeDtypeStruct((S, H * QK_PAD_DIM), BF16),
                   jax.ShapeDtypeStruct((S // blk, H, VT_ROWS, blk), BF16)],
        compiler_params=_params(("arbitrary", "arbitrary"), pipelined, 3 * _nbytes((blk, bn), F32)),
        name="kv_up",
    )(ckvn, w_ukv, kr)


ATTN_BK = 512
ATTN_HEADS = 2


def _attn_kernel(q_ref, k_ref, vt_ref, ga_ref, ct_ref, o_ref, s_sc, m_sc, acc_sc, *, bk, heads):
    qi = pl.program_id(1)
    bq = 2 * bk
    m_sc[...] = jnp.full(m_sc.shape, -jnp.inf, F32)
    acc_sc[...] = jnp.zeros(acc_sc.shape, F32)

    def scores(j, slot):
        off = pl.multiple_of(j * bk, bk)
        for h in range(heads):
            cols = slice(h * QK_PAD_DIM, (h + 1) * QK_PAD_DIM)
            s_sc[slot, h] = lax.dot_general(k_ref[pl.ds(off, bk), cols], q_ref[:, cols],
                                            (((1,), (1,)), ((), ())), preferred_element_type=F32)

    def consume(j, slot, mask):
        for h in range(heads):
            s = s_sc[slot, h]
            if mask is not None:
                s = jnp.where(mask, s, -jnp.inf)
            m_prev = m_sc[h]
            m_new = jnp.maximum(m_prev, jnp.max(s, axis=0, keepdims=True))
            alpha = jnp.exp2(m_prev - m_new)
            p = jnp.exp2(s - m_new).astype(BF16)
            acc_sc[h] = alpha * acc_sc[h] + jnp.dot(vt_ref[j, h], p, preferred_element_type=F32)
            m_sc[h] = m_new

    def pair(t, carry):
        scores(2 * t + 1, 1)
        consume(2 * t, 0, None)
        scores(2 * t + 2, 0)
        consume(2 * t + 1, 1, None)
        return carry

    scores(0, 0)
    lax.fori_loop(0, qi, pair, 0)

    shift = CHUNK.bit_length() - 1
    kc = lax.broadcasted_iota(jnp.int32, (bk, bq), 0) >> shift
    qc = lax.broadcasted_iota(jnp.int32, (bk, bq), 1) >> shift
    scores(2 * qi + 1, 1)
    consume(2 * qi, 0, kc <= qc)
    consume(2 * qi + 1, 1, kc + (bk >> shift) <= qc)
    for h in range(heads):
        cols = slice(h * V_HEAD_DIM, (h + 1) * V_HEAD_DIM)
        attn = (acc_sc[h, :V_HEAD_DIM, :] / acc_sc[h, V_HEAD_DIM:V_HEAD_DIM + 1, :]).T
        o_ref[:, cols] = (ga_ref[:, cols].astype(F32) * attn + ct_ref[:, cols].astype(F32)).astype(BF16)


def _attention(q, k, vt, g_attn, conv_term):
    S = q.shape[0]
    H = q.shape[1] // QK_PAD_DIM
    nblk, _, vrows, bk = vt.shape
    bq = 2 * bk
    heads = math.gcd(H, ATTN_HEADS)
    assert vrows == VT_ROWS and bk % CHUNK == 0 and CHUNK & (CHUNK - 1) == 0 and S % bq == 0
    hv = pl.BlockSpec((bq, heads * V_HEAD_DIM), lambda h, i: (i, h))
    pipelined = heads * (_nbytes((bq, QK_PAD_DIM), BF16) + _nbytes((S, QK_PAD_DIM), BF16)
                         + _nbytes((S, VT_ROWS), BF16) + 3 * _nbytes((bq, V_HEAD_DIM), BF16))
    return pl.pallas_call(
        functools.partial(_attn_kernel, bk=bk, heads=heads),
        grid=(H // heads, S // bq),
        in_specs=[pl.BlockSpec((bq, heads * QK_PAD_DIM), lambda h, i: (i, h)),
                  pl.BlockSpec((S, heads * QK_PAD_DIM), lambda h, i: (0, h)),
                  pl.BlockSpec((nblk, heads, VT_ROWS, bk), lambda h, i: (0, h, 0, 0)),
                  hv, hv],
        out_specs=hv,
        out_shape=jax.ShapeDtypeStruct((S, H * V_HEAD_DIM), BF16),
        scratch_shapes=[pltpu.VMEM((2, heads, bk, bq), F32),
                        pltpu.VMEM((heads, 1, bq), F32),
                        pltpu.VMEM((heads, VT_ROWS, bq), F32)],
        compiler_params=_params(("arbitrary", "arbitrary"), pipelined,
                                (2 * heads + 4) * _nbytes((bk, bq), F32)),
        name="attention",
    )(q, k, vt, g_attn, conv_term)


def _mm_kernel(x_ref, w_ref, o_ref):
    kk = pl.program_id(2)

    @pl.when(kk == 0)
    def _():
        o_ref[...] = jnp.dot(x_ref[...], w_ref[...], preferred_element_type=F32)

    @pl.when(kk > 0)
    def _():
        o_ref[...] += jnp.dot(x_ref[...], w_ref[...], preferred_element_type=F32)


def _matmul(x, w, layer):
    M, K = x.shape
    N = w.shape[-1]
    bm, bn, bk = _blk(M, 1024), _blk(N, 1024), _blk(K, 4096)
    pipelined = _nbytes((bm, bk), BF16) + _nbytes((bk, bn), BF16) + _nbytes((bm, bn), F32)
    return pl.pallas_call(
        _mm_kernel,
        grid=(M // bm, N // bn, K // bk),
        in_specs=[pl.BlockSpec((bm, bk), lambda m, n, k: (m, k)),
                  pl.BlockSpec((None, bk, bn), lambda m, n, k: (layer, k, n))],
        out_specs=pl.BlockSpec((bm, bn), lambda m, n, k: (m, n)),
        out_shape=jax.ShapeDtypeStruct((M, N), F32),
        compiler_params=_params(("arbitrary", "arbitrary", "arbitrary"), pipelined, _nbytes((bm, bn), F32)),
        name="matmul",
    )(x, w)


def _ln_kernel(x_ref, f_ref, g_ref, b_ref, o32_ref, o16_ref, *, alpha):
    y = alpha * x_ref[...] + f_ref[...]
    mu = jnp.mean(y, axis=-1, keepdims=True)
    d = y - mu
    var = jnp.mean(d * d, axis=-1, keepdims=True)
    out = d * lax.rsqrt(var + LN_EPS) * g_ref[...] + b_ref[...]
    o32_ref[...] = out
    o16_ref[...] = out.astype(BF16)


def _deepnorm_ln(x, f, g, b, alpha):
    S, D = x.shape
    bm = _blk(S, 256)
    row = pl.BlockSpec((bm, D), lambda i: (i, 0))
    vec = pl.BlockSpec((1, D), lambda i: (0, 0))
    pipelined = 3 * _nbytes((bm, D), F32) + _nbytes((bm, D), BF16)
    return pl.pallas_call(
        functools.partial(_ln_kernel, alpha=alpha),
        grid=(S // bm,),
        in_specs=[row, row, vec, vec],
        out_specs=[row, row],
        out_shape=[jax.ShapeDtypeStruct((S, D), F32), jax.ShapeDtypeStruct((S, D), BF16)],
        compiler_params=_params(("arbitrary",), pipelined, 3 * _nbytes((bm, D), F32)),
        name="deepnorm_ln",
    )(x, f, g.reshape(1, -1), b.reshape(1, -1))


FFN_BN = 512
FFN_BM = 1024


def _ffn_in_kernel(x_ref, wa_ref, wu_ref, cw_ref, o_ref, tail_ref):
    _reset_conv_tail(tail_ref, pl.program_id(1) == 0)
    hm = x_ref.shape[0] // ROW_SPLIT
    for r in range(ROW_SPLIT):
        rows = slice(r * hm, (r + 1) * hm)
        x = x_ref[rows, :]
        a = jnp.dot(x, wa_ref[...], preferred_element_type=F32)
        conv = _causal_conv3(a, cw_ref, tail_ref)
        u = jnp.dot(x, wu_ref[...], preferred_element_type=F32)
        o_ref[rows, :] = (conv * _sigmoid(conv) * u).astype(BF16)


def _ffn_in(xb, w_ffn_in, layer, ffn_conv_w):
    S, D = xb.shape
    FF = ffn_conv_w.shape[-1]
    assert w_ffn_in.shape[1:] == (D, 2 * FF)
    bm = _blk(S, FFN_BM)
    bn = _blk(FF, FFN_BN)
    nt = FF // bn
    pipelined = _nbytes((bm, D), BF16) + 2 * _nbytes((D, bn), BF16) + _nbytes((bm, bn), BF16)
    return pl.pallas_call(
        _ffn_in_kernel,
        grid=(nt, S // bm),
        in_specs=[pl.BlockSpec((bm, D), lambda n, m: (m, 0)),
                  pl.BlockSpec((None, D, bn), lambda n, m: (layer, 0, n)),
                  pl.BlockSpec((None, D, bn), lambda n, m: (layer, 0, nt + n)),
                  pl.BlockSpec((CONV_WIDTH, bn), lambda n, m: (0, n))],
        out_specs=pl.BlockSpec((bm, bn), lambda n, m: (m, n)),
        out_shape=jax.ShapeDtypeStruct((S, FF), BF16),
        scratch_shapes=[pltpu.VMEM((V7X_SUBLANES, bn), F32)],
        compiler_params=_params(("arbitrary", "arbitrary"), pipelined, 6 * _nbytes((bm, bn), F32)),
        name="ffn_in",
    )(xb, w_ffn_in, w_ffn_in, ffn_conv_w)


def kernel(x, positions, w_in, b_gate, q_norm_w, kv_norm_w, w_uq, w_ukv, conv_w, w_o, ln1_g, ln1_b,
           w_ffn_in, ffn_conv_w, w_ffn_down, ln2_g, ln2_b):
    B, S, D = x.shape
    assert B == 1, "kernels are written for a single sequence"
    depth = w_in.shape[0]
    q_lora = q_norm_w.shape[-1]
    kv_lora = kv_norm_w.shape[-1]
    n_lat = q_lora + kv_lora + QK_ROPE_DIM
    assert (q_lora + kv_lora) % V7X_LANES == 0
    H = w_uq.shape[-1] // QK_HEAD_DIM
    alpha = (2.0 * depth) ** 0.25

    w_lat = jnp.pad(w_in[:, :, :n_lat], ((0, 0), (0, 0), (0, V7X_LANES - QK_ROPE_DIM))).astype(BF16)
    w_br = _regroup_cast(jnp.swapaxes(w_in, 1, 2), n_lat, BRANCH_GROUPS, D, _blk(D, BRANCH_BN))
    w_uq_pad = jnp.pad(w_uq.reshape(depth, q_lora, H, QK_HEAD_DIM),
                       ((0, 0), (0, 0), (0, 0), (0, QK_PAD_DIM - QK_HEAD_DIM))
                       ).reshape(depth, q_lora, H * QK_PAD_DIM).astype(BF16)
    w_ukv_b = w_ukv.astype(BF16)
    w_o_b = w_o.astype(BF16)
    w_ffn_in_b = w_ffn_in.astype(BF16)
    w_ffn_down_b = w_ffn_down.astype(BF16)

    tabs = _rope_tables(positions.reshape(S, 1))
    xf = x.reshape(S, D)
    xb = xf.astype(BF16)
    attn_blk = _blk(S, ATTN_BK)
    for l in range(depth):
        qn, ckvn, kr = _latent_proj(xb, w_lat, l, q_norm_w[l], kv_norm_w[l], tabs)
        conv_term, g_attn = _branch_proj(xb, w_br, l, b_gate[l].reshape(1, -1), conv_w[l])
        q = _q_up(qn, w_uq_pad, l, tabs)
        k, vt = _kv_up(ckvn, w_ukv_b, l, kr, attn_blk)
        merged = _attention(q, k, vt, g_attn, conv_term)
        xf, xb = _deepnorm_ln(xf, _matmul(merged, w_o_b, l), ln1_g[l], ln1_b[l], alpha)
        g = _ffn_in(xb, w_ffn_in_b, l, ffn_conv_w[l])
        xf, xb = _deepnorm_ln(xf, _matmul(g, w_ffn_down_b, l), ln2_g[l], ln2_b[l], alpha)
    return xf.reshape(B, S, D)
```

```python
import functools
import math

import numpy as np
import jax
import jax.numpy as jnp
from jax import lax
from jax.experimental import pallas as pl
from jax.experimental.pallas import tpu as pltpu

F32 = jnp.float32
BF16 = jnp.bfloat16

CHUNK = 64
V_HEAD_DIM = 128
QK_NOPE_DIM = 128
QK_ROPE_DIM = 64
QK_HEAD_DIM = QK_NOPE_DIM + QK_ROPE_DIM
ROPE_THETA = 10000.0
CONV_WIDTH = 3
LN_EPS = 1e-5
RMS_EPS = 1e-6

V7X_LANES = 128
V7X_SUBLANES = 8
V7X_VMEM_BYTES = 64 * 1024 * 1024
V7X_VMEM_RESERVE = 6 * 1024 * 1024

QK_PAD_DIM = 2 * V7X_LANES


def _vmem_limit(pipelined_bytes, resident_bytes):
    want = 2 * pipelined_bytes + resident_bytes + V7X_VMEM_RESERVE
    return int(min(V7X_VMEM_BYTES - V7X_VMEM_RESERVE, max(want, 16 * 1024 * 1024)))


def _nbytes(shape, dtype):
    return int(np.prod(shape)) * jnp.dtype(dtype).itemsize


def _params(semantics, pipelined_bytes, resident_bytes):
    return pltpu.CompilerParams(dimension_semantics=semantics,
                                vmem_limit_bytes=_vmem_limit(pipelined_bytes, resident_bytes))


def _blk(dim, want):
    b = min(dim, want)
    assert dim % b == 0, (dim, want)
    return b


def _sigmoid(v):
    return 1.0 / (1.0 + jnp.exp(-v))


def _rope_tables_kernel(pos_ref, invf_ref, c_ref, a_ref, b_ref):
    half = QK_ROPE_DIM // 2
    ang = pos_ref[...].astype(F32) * invf_ref[...]
    lane = lax.broadcasted_iota(jnp.int32, ang.shape, 1)
    c = jnp.cos(ang)
    s = jnp.sin(ang)
    c_ref[...] = jnp.where(lane < 2 * half, c, 0.0)
    a_ref[...] = jnp.where(lane < half, -s, 0.0)
    b_ref[...] = jnp.where((lane >= half) & (lane < 2 * half), s, 0.0)


def _rope_tables(positions_col):
    S = positions_col.shape[0]
    half = QK_ROPE_DIM // 2
    inv = (np.float32(1.0) /
           (np.float32(ROPE_THETA) ** (np.arange(0, QK_ROPE_DIM, 2, dtype=np.float32) / np.float32(QK_ROPE_DIM))))
    row = np.zeros((1, V7X_LANES), np.float32)
    row[0, :half] = inv
    row[0, half:2 * half] = inv
    bs = _blk(S, 1024)
    tab = jax.ShapeDtypeStruct((S, V7X_LANES), F32)
    spec = pl.BlockSpec((bs, V7X_LANES), lambda i: (i, 0))
    return pl.pallas_call(
        _rope_tables_kernel,
        grid=(S // bs,),
        in_specs=[pl.BlockSpec((bs, 1), lambda i: (i, 0)),
                  pl.BlockSpec((1, V7X_LANES), lambda i: (0, 0))],
        out_specs=[spec, spec, spec],
        out_shape=[tab, tab, tab],
        name="rope_tables",
    )(positions_col, jnp.asarray(row))


def _rope_tile(r, c, a, b):
    return r * c + pltpu.roll(r, 96, axis=1) * a + pltpu.roll(r, 32, axis=1) * b


def _latent_kernel(x_ref, w_ref, qw_ref, kvw_ref, c_ref, a_ref, b_ref,
                   qn_ref, ckvn_ref, kr_ref, *, q_lora, kv_lora):
    def rms(v, w):
        return v * lax.rsqrt(jnp.mean(v * v, axis=-1, keepdims=True) + RMS_EPS) * w

    hm = x_ref.shape[0] // ROW_SPLIT
    for r in range(ROW_SPLIT):
        rows = slice(r * hm, (r + 1) * hm)
        acc = jnp.dot(x_ref[rows, :], w_ref[...], preferred_element_type=F32)
        qn_ref[rows, :] = rms(acc[:, :q_lora], qw_ref[...]).astype(BF16)
        ckvn_ref[rows, :] = rms(acc[:, q_lora:q_lora + kv_lora], kvw_ref[...]).astype(BF16)
        kr = acc[:, q_lora + kv_lora:]
        kr_ref[rows, :] = _rope_tile(kr, c_ref[rows, :], a_ref[rows, :], b_ref[rows, :]).astype(BF16)


def _latent_proj(xb, w_lat, layer, q_norm_w, kv_norm_w, tabs):
    S, D = xb.shape
    q_lora = q_norm_w.shape[-1]
    kv_lora = kv_norm_w.shape[-1]
    n_pad = w_lat.shape[-1]
    assert n_pad == q_lora + kv_lora + V7X_LANES
    bm = _blk(S, 512)
    row = lambda i: (i, 0)
    const = lambda i: (0, 0)
    tab_spec = pl.BlockSpec((bm, V7X_LANES), row)
    pipelined = (_nbytes((bm, D), BF16) + _nbytes((D, n_pad), BF16)
                 + 3 * _nbytes((bm, V7X_LANES), F32) + _nbytes((bm, n_pad), BF16))
    return pl.pallas_call(
        functools.partial(_latent_kernel, q_lora=q_lora, kv_lora=kv_lora),
        grid=(S // bm,),
        in_specs=[pl.BlockSpec((bm, D), row),
                  pl.BlockSpec((None, D, n_pad), lambda i: (layer, 0, 0)),
                  pl.BlockSpec((1, q_lora), const),
                  pl.BlockSpec((1, kv_lora), const),
                  tab_spec, tab_spec, tab_spec],
        out_specs=[pl.BlockSpec((bm, q_lora), row),
                   pl.BlockSpec((bm, kv_lora), row),
                   pl.BlockSpec((bm, V7X_LANES), row)],
        out_shape=[jax.ShapeDtypeStruct((S, q_lora), BF16),
                   jax.ShapeDtypeStruct((S, kv_lora), BF16),
                   jax.ShapeDtypeStruct((S, V7X_LANES), BF16)],
        compiler_params=_params(("arbitrary",), pipelined, 3 * _nbytes((bm, n_pad), F32)),
        name="latent_proj",
    )(xb, w_lat, q_norm_w.reshape(1, -1), kv_norm_w.reshape(1, -1), *tabs)


ROW_SPLIT = 2


def _reset_conv_tail(tail_ref, first_tile):
    @pl.when(first_tile)
    def _():
        tail_ref[...] = jnp.zeros_like(tail_ref)


def _causal_conv3(z, w_ref, tail_ref):
    bm, bn = z.shape
    prev1 = tail_ref[V7X_SUBLANES - 1:V7X_SUBLANES, :]
    prev2 = tail_ref[V7X_SUBLANES - 2:V7X_SUBLANES - 1, :]
    row = lax.broadcasted_iota(jnp.int32, (bm, bn), 0)
    z1 = jnp.where(row == 0, prev1, pltpu.roll(z, 1, axis=0))
    z2 = pltpu.roll(z, 2, axis=0)
    z2 = jnp.where(row == 0, prev2, jnp.where(row == 1, prev1, z2))
    tail_ref[...] = z[bm - V7X_SUBLANES:, :]
    return z2 * w_ref[0:1, :] + z1 * w_ref[1:2, :] + z * w_ref[2:3, :]


BRANCH_GROUPS = 5
BRANCH_BN = 256
BRANCH_BM = 1024
REGROUP_BD = 1024


def _regroup_kernel(*refs, groups, bn):
    o_ref = refs[-1]
    for g in range(groups):
        o_ref[:, g * bn:(g + 1) * bn] = refs[g][0].T.astype(BF16)


def _regroup_cast(w_t, col0, groups, C, bn):
    depth, _, D = w_t.shape
    nt = C // bn
    bd = _blk(D, REGROUP_BD)
    assert col0 % V7X_SUBLANES == 0

    def in_spec(g):
        return pl.BlockSpec((pl.Element(1), pl.Element(bn), pl.Element(bd)),
                            lambda l, n, d: (l, pl.multiple_of(col0 + g * C + n * bn, V7X_SUBLANES),
                                             pl.multiple_of(d * bd, V7X_LANES)))

    pipelined = groups * _nbytes((bn, bd), F32) + _nbytes((bd, groups * bn), BF16)
    return pl.pallas_call(
        functools.partial(_regroup_kernel, groups=groups, bn=bn),
        grid=(depth, nt, D // bd),
        in_specs=[in_spec(g) for g in range(groups)],
        out_specs=pl.BlockSpec((None, None, bd, groups * bn), lambda l, n, d: (l, n, d, 0)),
        out_shape=jax.ShapeDtypeStruct((depth, nt, D, groups * bn), BF16),
        compiler_params=_params(("arbitrary", "arbitrary", "arbitrary"), pipelined,
                                2 * _nbytes((bd, groups * bn), F32)),
        name="regroup_cast",
    )(*([w_t] * groups))


def _branch_kernel(x_ref, w_ref, ba_ref, bc_ref, cw_ref, ct_ref, ga_ref, tail_ref):
    bn = ct_ref.shape[1]
    _reset_conv_tail(tail_ref, pl.program_id(1) == 0)
    hm = x_ref.shape[0] // ROW_SPLIT
    for r in range(ROW_SPLIT):
        rows = slice(r * hm, (r + 1) * hm)
        acc = jnp.dot(x_ref[rows, :], w_ref[...], preferred_element_type=F32)
        cb, cc, ch, ga, gc = (acc[:, g * bn:(g + 1) * bn] for g in range(BRANCH_GROUPS))
        conv = _causal_conv3(cc * ch, cw_ref, tail_ref)
        ct_ref[rows, :] = (_sigmoid(gc + bc_ref[...]) * cb * conv).astype(BF16)
        ga_ref[rows, :] = _sigmoid(ga + ba_ref[...]).astype(BF16)


def _branch_proj(xb, w_br, layer, b_gate, conv_w):
    S, D = xb.shape
    C = conv_w.shape[-1]
    _, nt, _, wn = w_br.shape
    bn = wn // BRANCH_GROUPS
    assert nt * bn == C and b_gate.shape == (1, 2 * C)
    bm = _blk(S, BRANCH_BM)
    out_spec = pl.BlockSpec((bm, bn), lambda n, m: (m, n))
    pipelined = (_nbytes((bm, D), BF16) + _nbytes((D, wn), BF16) + 2 * _nbytes((bm, bn), BF16))
    return pl.pallas_call(
        _branch_kernel,
        grid=(nt, S // bm),
        in_specs=[pl.BlockSpec((bm, D), lambda n, m: (m, 0)),
                  pl.BlockSpec((None, None, D, wn), lambda n, m: (layer, n, 0, 0)),
                  pl.BlockSpec((1, bn), lambda n, m: (0, n)),
                  pl.BlockSpec((1, bn), lambda n, m: (0, nt + n)),
                  pl.BlockSpec((CONV_WIDTH, bn), lambda n, m: (0, n))],
        out_specs=[out_spec, out_spec],
        out_shape=[jax.ShapeDtypeStruct((S, C), BF16), jax.ShapeDtypeStruct((S, C), BF16)],
        scratch_shapes=[pltpu.VMEM((V7X_SUBLANES, bn), F32)],
        compiler_params=_params(("arbitrary", "arbitrary"), pipelined, 3 * _nbytes((bm, wn), F32)),
        name="branch_proj",
    )(xb, w_br, b_gate, b_gate, conv_w)


def _qup_kernel(x_ref, w_ref, c_ref, a_ref, b_ref, o_ref, *, scale):
    hm = x_ref.shape[0] // ROW_SPLIT
    for r in range(ROW_SPLIT):
        rows = slice(r * hm, (r + 1) * hm)
        acc = jnp.dot(x_ref[rows, :], w_ref[...], preferred_element_type=F32)
        c, a, b = c_ref[rows, :], a_ref[rows, :], b_ref[rows, :]
        for h in range(acc.shape[1] // QK_PAD_DIM):
            lo = h * QK_PAD_DIM
            mid = lo + QK_NOPE_DIM
            o_ref[rows, lo:mid] = (acc[:, lo:mid] * scale).astype(BF16)
            o_ref[rows, mid:lo + QK_PAD_DIM] = (
                _rope_tile(acc[:, mid:lo + QK_PAD_DIM], c, a, b) * scale).astype(BF16)


def _q_up(qn, w_uq_pad, layer, tabs):
    S, K = qn.shape
    N = w_uq_pad.shape[-1]
    bm = _blk(S, 1024)
    bn = _blk(N, 1024)
    tab_spec = pl.BlockSpec((bm, V7X_LANES), lambda m, n: (m, 0))
    pipelined = (_nbytes((bm, K), BF16) + _nbytes((K, bn), BF16) + 3 * _nbytes((bm, V7X_LANES), F32)
                 + _nbytes((bm, bn), BF16))
    return pl.pallas_call(
        functools.partial(_qup_kernel, scale=math.log2(math.e) / math.sqrt(QK_HEAD_DIM)),
        grid=(S // bm, N // bn),
        in_specs=[pl.BlockSpec((bm, K), lambda m, n: (m, 0)),
                  pl.BlockSpec((None, K, bn), lambda m, n: (layer, 0, n)),
                  tab_spec, tab_spec, tab_spec],
        out_specs=pl.BlockSpec((bm, bn), lambda m, n: (m, n)),
        out_shape=jax.ShapeDtypeStruct((S, N), BF16),
        compiler_params=_params(("arbitrary", "arbitrary"), pipelined, 2 * _nbytes((bm, bn), F32)),
        name="q_up",
    )(qn, w_uq_pad, *tabs)


BF16_SUBLANE_ROWS = 2 * V7X_SUBLANES
VT_ROWS = V_HEAD_DIM + BF16_SUBLANE_ROWS


def _kvup_kernel(x_ref, w_ref, kr_ref, k_ref, vt_ref):
    acc = jnp.dot(x_ref[...], w_ref[...], preferred_element_type=F32)
    kr = kr_ref[...]
    hw = QK_NOPE_DIM + V_HEAD_DIM
    ones = jnp.ones((BF16_SUBLANE_ROWS, acc.shape[0]), BF16)
    for h in range(acc.shape[1] // hw):
        k_ref[:, h * QK_PAD_DIM:h * QK_PAD_DIM + QK_NOPE_DIM] = acc[:, h * hw:h * hw + QK_NOPE_DIM].astype(BF16)
        k_ref[:, h * QK_PAD_DIM + QK_NOPE_DIM:(h + 1) * QK_PAD_DIM] = kr
        vt_ref[h, :V_HEAD_DIM, :] = acc[:, h * hw + QK_NOPE_DIM:(h + 1) * hw].T.astype(BF16)
        vt_ref[h, V_HEAD_DIM:, :] = ones


def _kv_up(ckvn, w_ukv, layer, kr, blk):
    S, K = ckvn.shape
    N = w_ukv.shape[-1]
    hw = QK_NOPE_DIM + V_HEAD_DIM
    H = N // hw
    bn = _blk(N, 2048)
    hb = bn // hw
    pipelined = (_nbytes((blk, K), BF16) + _nbytes((K, bn), BF16) + _nbytes((blk, V7X_LANES), BF16)
                 + _nbytes((blk, hb * (QK_PAD_DIM + VT_ROWS)), BF16))
    return pl.pallas_call(
        _kvup_kernel,
        grid=(S // blk, N // bn),
        in_specs=[pl.BlockSpec((blk, K), lambda m, n: (m, 0)),
                  pl.BlockSpec((None, K, bn), lambda m, n: (layer, 0, n)),
                  pl.BlockSpec((blk, V7X_LANES), lambda m, n: (m, 0))],
        out_specs=[pl.BlockSpec((blk, hb * QK_PAD_DIM), lambda m, n: (m, n)),
                   pl.BlockSpec((None, hb, VT_ROWS, blk), lambda m, n: (m, n, 0, 0))],
        out_shape=[jax.ShapeDtypeStruct((S, H * QK_PAD_DIM), BF16),
                   jax.ShapeDtypeStruct((S // blk, H, VT_ROWS, blk), BF16)],
        compiler_params=_params(("arbitrary", "arbitrary"), pipelined, 3 * _nbytes((blk, bn), F32)),
        name="kv_up",
    )(ckvn, w_ukv, kr)


ATTN_BK = 512
ATTN_HEADS = 1
ATTN_KPQ = 4


def _attn_kernel(q_ref, k_ref, vt_ref, ga_ref, ct_ref, o_ref, s_sc, smax_sc, m_sc, acc_sc, *, bk, kpq, heads):
    qi = pl.program_id(1)
    bq = kpq * bk
    m_sc[...] = jnp.full(m_sc.shape, -jnp.inf, F32)
    acc_sc[...] = jnp.zeros(acc_sc.shape, F32)

    everything = slice(0, bq)

    def scores(j, slot, qs=everything):
        off = pl.multiple_of(j * bk, bk)
        for h in range(heads):
            cols = slice(h * QK_PAD_DIM, (h + 1) * QK_PAD_DIM)
            s = lax.dot_general(k_ref[pl.ds(off, bk), cols], q_ref[qs, cols],
                                (((1,), (1,)), ((), ())), preferred_element_type=F32)
            s_sc[slot, h, :, qs] = s
            smax_sc[slot, h, :, qs] = jnp.max(s, axis=0, keepdims=True)

    def consume(j, slot, mask=None, qs=everything):
        for h in range(heads):
            s = s_sc[slot, h, :, qs]
            if mask is None:
                s_max = smax_sc[slot, h, :, qs]
            else:
                s = jnp.where(mask, s, -jnp.inf)
                s_max = jnp.max(s, axis=0, keepdims=True)
            m_prev = m_sc[h, :, qs]
            m_new = jnp.maximum(m_prev, s_max)
            alpha = jnp.exp2(m_prev - m_new)
            p = jnp.exp2(s - m_new).astype(BF16)
            acc_sc[h, :, qs] = alpha * acc_sc[h, :, qs] + jnp.dot(vt_ref[j, h], p, preferred_element_type=F32)
            m_sc[h, :, qs] = m_new

    def pair(t, carry):
        scores(2 * t + 1, 1)
        consume(2 * t, 0)
        scores(2 * t + 2, 0)
        consume(2 * t + 1, 1)
        return carry

    scores(0, 0)
    base = kpq * qi
    lax.fori_loop(0, base // 2, pair, 0)

    shift = CHUNK.bit_length() - 1
    tri = ((lax.broadcasted_iota(jnp.int32, (bk, bk), 0) >> shift)
           <= (lax.broadcasted_iota(jnp.int32, (bk, bk), 1) >> shift))
    for d in range(kpq):
        if d + 1 < kpq:
            scores(base + d + 1, (d + 1) % 2, slice((d + 1) * bk, bq))
        consume(base + d, d % 2, tri, slice(d * bk, (d + 1) * bk))
        if d + 1 < kpq:
            consume(base + d, d % 2, None, slice((d + 1) * bk, bq))
    for h in range(heads):
        cols = slice(h * V_HEAD_DIM, (h + 1) * V_HEAD_DIM)
        attn = (acc_sc[h, :V_HEAD_DIM, :] / acc_sc[h, V_HEAD_DIM:V_HEAD_DIM + 1, :]).T
        o_ref[:, cols] = (ga_ref[:, cols].astype(F32) * attn + ct_ref[:, cols].astype(F32)).astype(BF16)


def _attention(q, k, vt, g_attn, conv_term):
    S = q.shape[0]
    H = q.shape[1] // QK_PAD_DIM
    nblk, _, vrows, bk = vt.shape
    kpq = ATTN_KPQ if S % (ATTN_KPQ * bk) == 0 else 2
    bq = kpq * bk
    heads = math.gcd(H, ATTN_HEADS)
    assert vrows == VT_ROWS and bk % CHUNK == 0 and CHUNK & (CHUNK - 1) == 0 and S % bq == 0 and kpq % 2 == 0
    hv = pl.BlockSpec((bq, heads * V_HEAD_DIM), lambda h, i: (i, h))
    pipelined = heads * (_nbytes((bq, QK_PAD_DIM), BF16) + _nbytes((S, QK_PAD_DIM), BF16)
                         + _nbytes((S, VT_ROWS), BF16) + 3 * _nbytes((bq, V_HEAD_DIM), BF16))
    return pl.pallas_call(
        functools.partial(_attn_kernel, bk=bk, kpq=kpq, heads=heads),
        grid=(H // heads, S // bq),
        in_specs=[pl.BlockSpec((bq, heads * QK_PAD_DIM), lambda h, i: (i, h)),
                  pl.BlockSpec((S, heads * QK_PAD_DIM), lambda h, i: (0, h)),
                  pl.BlockSpec((nblk, heads, VT_ROWS, bk), lambda h, i: (0, h, 0, 0)),
                  hv, hv],
        out_specs=hv,
        out_shape=jax.ShapeDtypeStruct((S, H * V_HEAD_DIM), BF16),
        scratch_shapes=[pltpu.VMEM((2, heads, bk, bq), F32),
                        pltpu.VMEM((2, heads, 1, bq), F32),
                        pltpu.VMEM((heads, 1, bq), F32),
                        pltpu.VMEM((heads, VT_ROWS, bq), F32)],
        compiler_params=_params(("arbitrary", "arbitrary"), pipelined,
                                (2 * heads + 4) * _nbytes((bk, bq), F32)),
        name="attention",
    )(q, k, vt, g_attn, conv_term)


def _mm_kernel(x_ref, w_ref, res_ref, o_ref, *, alpha):
    kk = pl.program_id(2)

    @pl.when(kk == 0)
    def _():
        o_ref[...] = alpha * res_ref[...] + jnp.dot(x_ref[...], w_ref[...], preferred_element_type=F32)

    @pl.when(kk > 0)
    def _():
        o_ref[...] += jnp.dot(x_ref[...], w_ref[...], preferred_element_type=F32)


def _matmul_residual(x, w, layer, res, alpha):
    M, K = x.shape
    N = w.shape[-1]
    bm, bn, bk = _blk(M, 1024), _blk(N, 1024), _blk(K, 4096)
    pipelined = _nbytes((bm, bk), BF16) + _nbytes((bk, bn), BF16) + 2 * _nbytes((bm, bn), F32)
    return pl.pallas_call(
        functools.partial(_mm_kernel, alpha=alpha),
        grid=(M // bm, N // bn, K // bk),
        in_specs=[pl.BlockSpec((bm, bk), lambda m, n, k: (m, k)),
                  pl.BlockSpec((None, bk, bn), lambda m, n, k: (layer, k, n)),
                  pl.BlockSpec((bm, bn), lambda m, n, k: (m, n))],
        out_specs=pl.BlockSpec((bm, bn), lambda m, n, k: (m, n)),
        out_shape=jax.ShapeDtypeStruct((M, N), F32),
        compiler_params=_params(("arbitrary", "arbitrary", "arbitrary"), pipelined, _nbytes((bm, bn), F32)),
        name="matmul",
    )(x, w, res)


def _ln_kernel(y_ref, g_ref, b_ref, o32_ref, o16_ref):
    y = y_ref[...]
    mu = jnp.mean(y, axis=-1, keepdims=True)
    d = y - mu
    var = jnp.mean(d * d, axis=-1, keepdims=True)
    out = d * lax.rsqrt(var + LN_EPS) * g_ref[...] + b_ref[...]
    o32_ref[...] = out
    o16_ref[...] = out.astype(BF16)


def _layernorm(y, g, b):
    S, D = y.shape
    bm = _blk(S, 256)
    row = pl.BlockSpec((bm, D), lambda i: (i, 0))
    vec = pl.BlockSpec((1, D), lambda i: (0, 0))
    pipelined = 2 * _nbytes((bm, D), F32) + _nbytes((bm, D), BF16)
    return pl.pallas_call(
        _ln_kernel,
        grid=(S // bm,),
        in_specs=[row, vec, vec],
        out_specs=[row, row],
        out_shape=[jax.ShapeDtypeStruct((S, D), F32), jax.ShapeDtypeStruct((S, D), BF16)],
        compiler_params=_params(("arbitrary",), pipelined, 3 * _nbytes((bm, D), F32)),
        name="layernorm",
    )(y, g.reshape(1, -1), b.reshape(1, -1))


FFN_BN = 512
FFN_BM = 1024


def _ffn_in_kernel(x_ref, wa_ref, wu_ref, cw_ref, o_ref, tail_ref):
    _reset_conv_tail(tail_ref, pl.program_id(1) == 0)
    hm = x_ref.shape[0] // ROW_SPLIT
    for r in range(ROW_SPLIT):
        rows = slice(r * hm, (r + 1) * hm)
        x = x_ref[rows, :]
        a = jnp.dot(x, wa_ref[...], preferred_element_type=F32)
        conv = _causal_conv3(a, cw_ref, tail_ref)
        u = jnp.dot(x, wu_ref[...], preferred_element_type=F32)
        o_ref[rows, :] = (conv * _sigmoid(conv) * u).astype(BF16)


def _ffn_in(xb, w_ffn_in, layer, ffn_conv_w):
    S, D = xb.shape
    FF = ffn_conv_w.shape[-1]
    assert w_ffn_in.shape[1:] == (D, 2 * FF)
    bm = _blk(S, FFN_BM)
    bn = _blk(FF, FFN_BN)
    nt = FF // bn
    pipelined = _nbytes((bm, D), BF16) + 2 * _nbytes((D, bn), BF16) + _nbytes((bm, bn), BF16)
    return pl.pallas_call(
        _ffn_in_kernel,
        grid=(nt, S // bm),
        in_specs=[pl.BlockSpec((bm, D), lambda n, m: (m, 0)),
                  pl.BlockSpec((None, D, bn), lambda n, m: (layer, 0, n)),
                  pl.BlockSpec((None, D, bn), lambda n, m: (layer, 0, nt + n)),
                  pl.BlockSpec((CONV_WIDTH, bn), lambda n, m: (0, n))],
        out_specs=pl.BlockSpec((bm, bn), lambda n, m: (m, n)),
        out_shape=jax.ShapeDtypeStruct((S, FF), BF16),
        scratch_shapes=[pltpu.VMEM((V7X_SUBLANES, bn), F32)],
        compiler_params=_params(("arbitrary", "arbitrary"), pipelined, 6 * _nbytes((bm, bn), F32)),
        name="ffn_in",
    )(xb, w_ffn_in, w_ffn_in, ffn_conv_w)


def kernel(x, positions, w_in, b_gate, q_norm_w, kv_norm_w, w_uq, w_ukv, conv_w, w_o, ln1_g, ln1_b,
           w_ffn_in, ffn_conv_w, w_ffn_down, ln2_g, ln2_b):
    B, S, D = x.shape
    assert B == 1, "kernels are written for a single sequence"
    depth = w_in.shape[0]
    q_lora = q_norm_w.shape[-1]
    kv_lora = kv_norm_w.shape[-1]
    n_lat = q_lora + kv_lora + QK_ROPE_DIM
    assert (q_lora + kv_lora) % V7X_LANES == 0
    H = w_uq.shape[-1] // QK_HEAD_DIM
    alpha = (2.0 * depth) ** 0.25

    w_lat = jnp.pad(w_in[:, :, :n_lat], ((0, 0), (0, 0), (0, V7X_LANES - QK_ROPE_DIM))).astype(BF16)
    w_br = _regroup_cast(jnp.swapaxes(w_in, 1, 2), n_lat, BRANCH_GROUPS, D, _blk(D, BRANCH_BN))
    w_uq_pad = jnp.pad(w_uq.reshape(depth, q_lora, H, QK_HEAD_DIM),
                       ((0, 0), (0, 0), (0, 0), (0, QK_PAD_DIM - QK_HEAD_DIM))
                       ).reshape(depth, q_lora, H * QK_PAD_DIM).astype(BF16)
    w_ukv_b = w_ukv.astype(BF16)
    w_o_b = w_o.astype(BF16)
    w_ffn_in_b = w_ffn_in.astype(BF16)
    w_ffn_down_b = w_ffn_down.astype(BF16)

    tabs = _rope_tables(positions.reshape(S, 1))
    xf = x.reshape(S, D)
    xb = xf.astype(BF16)
    attn_blk = _blk(S, ATTN_BK)
    for l in range(depth):
        qn, ckvn, kr = _latent_proj(xb, w_lat, l, q_norm_w[l], kv_norm_w[l], tabs)
        conv_term, g_attn = _branch_proj(xb, w_br, l, b_gate[l].reshape(1, -1), conv_w[l])
        q = _q_up(qn, w_uq_pad, l, tabs)
        k, vt = _kv_up(ckvn, w_ukv_b, l, kr, attn_blk)
        merged = _attention(q, k, vt, g_attn, conv_term)
        xf, xb = _layernorm(_matmul_residual(merged, w_o_b, l, xf, alpha), ln1_g[l], ln1_b[l])
        g = _ffn_in(xb, w_ffn_in_b, l, ffn_conv_w[l])
        xf, xb = _layernorm(_matmul_residual(g, w_ffn_down_b, l, xf, alpha), ln2_g[l], ln2_b[l])
    return xf.reshape(B, S, D)
```

```python
import functools
import math

import numpy as np
import jax
import jax.numpy as jnp
from jax import lax
from jax.experimental import pallas as pl
from jax.experimental.pallas import tpu as pltpu

F32 = jnp.float32
BF16 = jnp.bfloat16

CHUNK = 64
V_HEAD_DIM = 128
QK_NOPE_DIM = 128
QK_ROPE_DIM = 64
QK_HEAD_DIM = QK_NOPE_DIM + QK_ROPE_DIM
ROPE_THETA = 10000.0
CONV_WIDTH = 3
LN_EPS = 1e-5
RMS_EPS = 1e-6

V7X_LANES = 128
V7X_SUBLANES = 8
V7X_VMEM_BYTES = 64 * 1024 * 1024
V7X_VMEM_RESERVE = 6 * 1024 * 1024

QK_PAD_DIM = 2 * V7X_LANES


def _vmem_limit(pipelined_bytes, resident_bytes):
    want = 2 * pipelined_bytes + resident_bytes + V7X_VMEM_RESERVE
    return int(min(V7X_VMEM_BYTES - V7X_VMEM_RESERVE, max(want, 16 * 1024 * 1024)))


def _nbytes(shape, dtype):
    return int(np.prod(shape)) * jnp.dtype(dtype).itemsize


def _params(semantics, pipelined_bytes, resident_bytes):
    return pltpu.CompilerParams(dimension_semantics=semantics,
                                vmem_limit_bytes=_vmem_limit(pipelined_bytes, resident_bytes))


def _blk(dim, want):
    b = min(dim, want)
    assert dim % b == 0, (dim, want)
    return b


def _sigmoid(v):
    return 1.0 / (1.0 + jnp.exp(-v))


def _rope_tables_kernel(pos_ref, invf_ref, c_ref, a_ref, b_ref):
    half = QK_ROPE_DIM // 2
    ang = pos_ref[...].astype(F32) * invf_ref[...]
    lane = lax.broadcasted_iota(jnp.int32, ang.shape, 1)
    c = jnp.cos(ang)
    s = jnp.sin(ang)
    c_ref[...] = jnp.where(lane < 2 * half, c, 0.0)
    a_ref[...] = jnp.where(lane < half, -s, 0.0)
    b_ref[...] = jnp.where((lane >= half) & (lane < 2 * half), s, 0.0)


def _rope_tables(positions_col):
    S = positions_col.shape[0]
    half = QK_ROPE_DIM // 2
    inv = (np.float32(1.0) /
           (np.float32(ROPE_THETA) ** (np.arange(0, QK_ROPE_DIM, 2, dtype=np.float32) / np.float32(QK_ROPE_DIM))))
    row = np.zeros((1, V7X_LANES), np.float32)
    row[0, :half] = inv
    row[0, half:2 * half] = inv
    bs = _blk(S, 1024)
    tab = jax.ShapeDtypeStruct((S, V7X_LANES), F32)
    spec = pl.BlockSpec((bs, V7X_LANES), lambda i: (i, 0))
    return pl.pallas_call(
        _rope_tables_kernel,
        grid=(S // bs,),
        in_specs=[pl.BlockSpec((bs, 1), lambda i: (i, 0)),
                  pl.BlockSpec((1, V7X_LANES), lambda i: (0, 0))],
        out_specs=[spec, spec, spec],
        out_shape=[tab, tab, tab],
        name="rope_tables",
    )(positions_col, jnp.asarray(row))


def _rope_tile(r, c, a, b):
    return r * c + pltpu.roll(r, 96, axis=1) * a + pltpu.roll(r, 32, axis=1) * b


def _latent_kernel(x_ref, w_ref, qw_ref, kvw_ref, c_ref, a_ref, b_ref,
                   qn_ref, ckvn_ref, kr_ref, *, q_lora, kv_lora):
    def rms(v, w):
        return v * lax.rsqrt(jnp.mean(v * v, axis=-1, keepdims=True) + RMS_EPS) * w

    hm = x_ref.shape[0] // ROW_SPLIT
    for r in range(ROW_SPLIT):
        rows = slice(r * hm, (r + 1) * hm)
        acc = jnp.dot(x_ref[rows, :], w_ref[...], preferred_element_type=F32)
        qn_ref[rows, :] = rms(acc[:, :q_lora], qw_ref[...]).astype(BF16)
        ckvn_ref[rows, :] = rms(acc[:, q_lora:q_lora + kv_lora], kvw_ref[...]).astype(BF16)
        kr = acc[:, q_lora + kv_lora:]
        kr_ref[rows, :] = _rope_tile(kr, c_ref[rows, :], a_ref[rows, :], b_ref[rows, :]).astype(BF16)


def _latent_proj(xb, w_lat, layer, q_norm_w, kv_norm_w, tabs):
    S, D = xb.shape
    q_lora = q_norm_w.shape[-1]
    kv_lora = kv_norm_w.shape[-1]
    n_pad = w_lat.shape[-1]
    assert n_pad == q_lora + kv_lora + V7X_LANES
    bm = _blk(S, 512)
    row = lambda i: (i, 0)
    const = lambda i: (0, 0)
    tab_spec = pl.BlockSpec((bm, V7X_LANES), row)
    pipelined = (_nbytes((bm, D), BF16) + _nbytes((D, n_pad), BF16)
                 + 3 * _nbytes((bm, V7X_LANES), F32) + _nbytes((bm, n_pad), BF16))
    return pl.pallas_call(
        functools.partial(_latent_kernel, q_lora=q_lora, kv_lora=kv_lora),
        grid=(S // bm,),
        in_specs=[pl.BlockSpec((bm, D), row),
                  pl.BlockSpec((None, D, n_pad), lambda i: (layer, 0, 0)),
                  pl.BlockSpec((1, q_lora), const),
                  pl.BlockSpec((1, kv_lora), const),
                  tab_spec, tab_spec, tab_spec],
        out_specs=[pl.BlockSpec((bm, q_lora), row),
                   pl.BlockSpec((bm, kv_lora), row),
                   pl.BlockSpec((bm, V7X_LANES), row)],
        out_shape=[jax.ShapeDtypeStruct((S, q_lora), BF16),
                   jax.ShapeDtypeStruct((S, kv_lora), BF16),
                   jax.ShapeDtypeStruct((S, V7X_LANES), BF16)],
        compiler_params=_params(("arbitrary",), pipelined, 3 * _nbytes((bm, n_pad), F32)),
        name="latent_proj",
    )(xb, w_lat, q_norm_w.reshape(1, -1), kv_norm_w.reshape(1, -1), *tabs)


ROW_SPLIT = 2


def _reset_conv_tail(tail_ref, first_tile):
    @pl.when(first_tile)
    def _():
        tail_ref[...] = jnp.zeros_like(tail_ref)


def _causal_conv3(z, w_ref, tail_ref):
    bm, bn = z.shape
    prev1 = tail_ref[V7X_SUBLANES - 1:V7X_SUBLANES, :]
    prev2 = tail_ref[V7X_SUBLANES - 2:V7X_SUBLANES - 1, :]
    row = lax.broadcasted_iota(jnp.int32, (bm, bn), 0)
    z1 = jnp.where(row == 0, prev1, pltpu.roll(z, 1, axis=0))
    z2 = pltpu.roll(z, 2, axis=0)
    z2 = jnp.where(row == 0, prev2, jnp.where(row == 1, prev1, z2))
    tail_ref[...] = z[bm - V7X_SUBLANES:, :]
    return z2 * w_ref[0:1, :] + z1 * w_ref[1:2, :] + z * w_ref[2:3, :]


BRANCH_GROUPS = 5
BRANCH_BN = 256
BRANCH_BM = 1024
REGROUP_BD = 1024


def _regroup_kernel(*refs, groups, bn):
    o_ref = refs[-1]
    for g in range(groups):
        o_ref[:, g * bn:(g + 1) * bn] = refs[g][0].T.astype(BF16)


def _regroup_cast(w_t, col0, groups, C, bn):
    depth, _, D = w_t.shape
    nt = C // bn
    bd = _blk(D, REGROUP_BD)
    assert col0 % V7X_SUBLANES == 0

    def in_spec(g):
        return pl.BlockSpec((pl.Element(1), pl.Element(bn), pl.Element(bd)),
                            lambda l, n, d: (l, pl.multiple_of(col0 + g * C + n * bn, V7X_SUBLANES),
                                             pl.multiple_of(d * bd, V7X_LANES)))

    pipelined = groups * _nbytes((bn, bd), F32) + _nbytes((bd, groups * bn), BF16)
    return pl.pallas_call(
        functools.partial(_regroup_kernel, groups=groups, bn=bn),
        grid=(depth, nt, D // bd),
        in_specs=[in_spec(g) for g in range(groups)],
        out_specs=pl.BlockSpec((None, None, bd, groups * bn), lambda l, n, d: (l, n, d, 0)),
        out_shape=jax.ShapeDtypeStruct((depth, nt, D, groups * bn), BF16),
        compiler_params=_params(("arbitrary", "arbitrary", "arbitrary"), pipelined,
                                2 * _nbytes((bd, groups * bn), F32)),
        name="regroup_cast",
    )(*([w_t] * groups))


def _branch_kernel(x_ref, w_ref, ba_ref, bc_ref, cw_ref, ct_ref, ga_ref, tail_ref):
    bn = ct_ref.shape[1]
    _reset_conv_tail(tail_ref, pl.program_id(1) == 0)
    hm = x_ref.shape[0] // ROW_SPLIT
    for r in range(ROW_SPLIT):
        rows = slice(r * hm, (r + 1) * hm)
        acc = jnp.dot(x_ref[rows, :], w_ref[...], preferred_element_type=F32)
        cb, cc, ch, ga, gc = (acc[:, g * bn:(g + 1) * bn] for g in range(BRANCH_GROUPS))
        conv = _causal_conv3(cc * ch, cw_ref, tail_ref)
        ct_ref[rows, :] = (_sigmoid(gc + bc_ref[...]) * cb * conv).astype(BF16)
        ga_ref[rows, :] = _sigmoid(ga + ba_ref[...]).astype(BF16)


def _branch_proj(xb, w_br, layer, b_gate, conv_w):
    S, D = xb.shape
    C = conv_w.shape[-1]
    _, nt, _, wn = w_br.shape
    bn = wn // BRANCH_GROUPS
    assert nt * bn == C and b_gate.shape == (1, 2 * C)
    bm = _blk(S, BRANCH_BM)
    out_spec = pl.BlockSpec((bm, bn), lambda n, m: (m, n))
    pipelined = (_nbytes((bm, D), BF16) + _nbytes((D, wn), BF16) + 2 * _nbytes((bm, bn), BF16))
    return pl.pallas_call(
        _branch_kernel,
        grid=(nt, S // bm),
        in_specs=[pl.BlockSpec((bm, D), lambda n, m: (m, 0)),
                  pl.BlockSpec((None, None, D, wn), lambda n, m: (layer, n, 0, 0)),
                  pl.BlockSpec((1, bn), lambda n, m: (0, n)),
                  pl.BlockSpec((1, bn), lambda n, m: (0, nt + n)),
                  pl.BlockSpec((CONV_WIDTH, bn), lambda n, m: (0, n))],
        out_specs=[out_spec, out_spec],
        out_shape=[jax.ShapeDtypeStruct((S, C), BF16), jax.ShapeDtypeStruct((S, C), BF16)],
        scratch_shapes=[pltpu.VMEM((V7X_SUBLANES, bn), F32)],
        compiler_params=_params(("arbitrary", "arbitrary"), pipelined, 3 * _nbytes((bm, wn), F32)),
        name="branch_proj",
    )(xb, w_br, b_gate, b_gate, conv_w)


def _qup_kernel(x_ref, w_ref, c_ref, a_ref, b_ref, o_ref, *, scale):
    hm = x_ref.shape[0] // ROW_SPLIT
    for r in range(ROW_SPLIT):
        rows = slice(r * hm, (r + 1) * hm)
        acc = jnp.dot(x_ref[rows, :], w_ref[...], preferred_element_type=F32)
        c, a, b = c_ref[rows, :], a_ref[rows, :], b_ref[rows, :]
        for h in range(acc.shape[1] // QK_PAD_DIM):
            lo = h * QK_PAD_DIM
            mid = lo + QK_NOPE_DIM
            o_ref[rows, lo:mid] = (acc[:, lo:mid] * scale).astype(BF16)
            o_ref[rows, mid:lo + QK_PAD_DIM] = (
                _rope_tile(acc[:, mid:lo + QK_PAD_DIM], c, a, b) * scale).astype(BF16)


def _q_up(qn, w_uq_pad, layer, tabs):
    S, K = qn.shape
    N = w_uq_pad.shape[-1]
    bm = _blk(S, 1024)
    bn = _blk(N, 1024)
    tab_spec = pl.BlockSpec((bm, V7X_LANES), lambda m, n: (m, 0))
    pipelined = (_nbytes((bm, K), BF16) + _nbytes((K, bn), BF16) + 3 * _nbytes((bm, V7X_LANES), F32)
                 + _nbytes((bm, bn), BF16))
    return pl.pallas_call(
        functools.partial(_qup_kernel, scale=math.log2(math.e) / math.sqrt(QK_HEAD_DIM)),
        grid=(S // bm, N // bn),
        in_specs=[pl.BlockSpec((bm, K), lambda m, n: (m, 0)),
                  pl.BlockSpec((None, K, bn), lambda m, n: (layer, 0, n)),
                  tab_spec, tab_spec, tab_spec],
        out_specs=pl.BlockSpec((bm, bn), lambda m, n: (m, n)),
        out_shape=jax.ShapeDtypeStruct((S, N), BF16),
        compiler_params=_params(("arbitrary", "arbitrary"), pipelined, 2 * _nbytes((bm, bn), F32)),
        name="q_up",
    )(qn, w_uq_pad, *tabs)


BF16_SUBLANE_ROWS = 2 * V7X_SUBLANES
VT_ROWS = V_HEAD_DIM + BF16_SUBLANE_ROWS


def _kvup_kernel(x_ref, w_ref, k_ref, vt_ref):
    acc = jnp.dot(x_ref[...], w_ref[...], preferred_element_type=F32)
    hw = QK_NOPE_DIM + V_HEAD_DIM
    ones = jnp.ones((BF16_SUBLANE_ROWS, acc.shape[0]), BF16)
    for h in range(acc.shape[1] // hw):
        k_ref[:, h * QK_NOPE_DIM:(h + 1) * QK_NOPE_DIM] = acc[:, h * hw:h * hw + QK_NOPE_DIM].astype(BF16)
        vt_ref[h, :V_HEAD_DIM, :] = acc[:, h * hw + QK_NOPE_DIM:(h + 1) * hw].T.astype(BF16)
        vt_ref[h, V_HEAD_DIM:, :] = ones


def _kv_up(ckvn, w_ukv, layer, blk):
    S, K = ckvn.shape
    N = w_ukv.shape[-1]
    hw = QK_NOPE_DIM + V_HEAD_DIM
    H = N // hw
    bn = _blk(N, 2048)
    hb = bn // hw
    pipelined = (_nbytes((blk, K), BF16) + _nbytes((K, bn), BF16)
                 + _nbytes((blk, hb * (QK_NOPE_DIM + VT_ROWS)), BF16))
    return pl.pallas_call(
        _kvup_kernel,
        grid=(S // blk, N // bn),
        in_specs=[pl.BlockSpec((blk, K), lambda m, n: (m, 0)),
                  pl.BlockSpec((None, K, bn), lambda m, n: (layer, 0, n))],
        out_specs=[pl.BlockSpec((blk, hb * QK_NOPE_DIM), lambda m, n: (m, n)),
                   pl.BlockSpec((None, hb, VT_ROWS, blk), lambda m, n: (m, n, 0, 0))],
        out_shape=[jax.ShapeDtypeStruct((S, H * QK_NOPE_DIM), BF16),
                   jax.ShapeDtypeStruct((S // blk, H, VT_ROWS, blk), BF16)],
        compiler_params=_params(("arbitrary", "arbitrary"), pipelined, 3 * _nbytes((blk, bn), F32)),
        name="kv_up",
    )(ckvn, w_ukv)


ATTN_BK = 512
ATTN_KPQ = 4


def _attn_kernel(q_ref, kn_ref, kr_ref, vt_ref, ga_ref, ct_ref, o_ref, qt_sc, s_sc, smax_sc, m_sc, acc_sc,
                 *, bk, kpq):
    qi = pl.program_id(1)
    bq = kpq * bk
    m_sc[...] = jnp.full(m_sc.shape, -jnp.inf, F32)
    acc_sc[...] = jnp.zeros(acc_sc.shape, F32)
    qt_sc[...] = q_ref[...].T

    def scores(j, slot, lo=0, hi=bq):
        off = pl.multiple_of(j * bk, bk)
        k = jnp.concatenate([kn_ref[pl.ds(off, bk), :], kr_ref[pl.ds(off, bk), :]], axis=1)
        s = jnp.dot(k, qt_sc[:, lo:hi], preferred_element_type=F32)
        s_sc[slot, :, lo:hi] = s
        smax_sc[slot, 0:1, lo:hi] = jnp.max(s, axis=0, keepdims=True)

    def consume(j, slot, mask=None, lo=0, hi=bq):
        s = s_sc[slot, :, lo:hi]
        if mask is None:
            s_max = smax_sc[slot, 0:1, lo:hi]
        else:
            s = jnp.where(mask, s, -jnp.inf)
            s_max = jnp.max(s, axis=0, keepdims=True)
        m_prev = m_sc[0:1, lo:hi]
        m_new = jnp.maximum(m_prev, s_max)
        alpha = jnp.exp2(m_prev - m_new)
        p = jnp.exp2(s - m_new).astype(BF16)
        acc_sc[:, lo:hi] = alpha * acc_sc[:, lo:hi] + jnp.dot(vt_ref[j], p, preferred_element_type=F32)
        m_sc[0:1, lo:hi] = m_new

    def run(t, carry):
        for u in range(kpq):
            scores(kpq * t + u + 1, (u + 1) % 2)
            consume(kpq * t + u, u % 2)
        return carry

    scores(0, 0)
    base = kpq * qi
    lax.fori_loop(0, qi, run, 0)

    shift = CHUNK.bit_length() - 1
    tri = ((lax.broadcasted_iota(jnp.int32, (bk, bk), 0) >> shift)
           <= (lax.broadcasted_iota(jnp.int32, (bk, bk), 1) >> shift))
    for d in range(kpq):
        if d + 1 < kpq:
            scores(base + d + 1, (d + 1) % 2, (d + 1) * bk, bq)
        consume(base + d, d % 2, tri, d * bk, (d + 1) * bk)
        if d + 1 < kpq:
            consume(base + d, d % 2, None, (d + 1) * bk, bq)
    attn = (acc_sc[:V_HEAD_DIM, :] / acc_sc[V_HEAD_DIM:V_HEAD_DIM + 1, :]).T
    o_ref[...] = (ga_ref[...].astype(F32) * attn + ct_ref[...].astype(F32)).astype(BF16)


def _attention(q, k_nope, kr, vt, g_attn, conv_term):
    S = q.shape[0]
    H = q.shape[1] // QK_PAD_DIM
    nblk, _, vrows, bk = vt.shape
    kpq = ATTN_KPQ if S % (ATTN_KPQ * bk) == 0 else 2
    bq = kpq * bk
    assert vrows == VT_ROWS and bk % CHUNK == 0 and CHUNK & (CHUNK - 1) == 0 and S % bq == 0 and kpq % 2 == 0
    hv = pl.BlockSpec((bq, V_HEAD_DIM), lambda h, i: (i, h))
    pipelined = (_nbytes((bq, QK_PAD_DIM), BF16) + _nbytes((S, QK_NOPE_DIM), BF16) + _nbytes((S, V7X_LANES), BF16)
                 + _nbytes((S, VT_ROWS), BF16) + 3 * _nbytes((bq, V_HEAD_DIM), BF16))
    return pl.pallas_call(
        functools.partial(_attn_kernel, bk=bk, kpq=kpq),
        grid=(H, S // bq),
        in_specs=[pl.BlockSpec((bq, QK_PAD_DIM), lambda h, i: (i, h)),
                  pl.BlockSpec((S, QK_NOPE_DIM), lambda h, i: (0, h)),
                  pl.BlockSpec((S, V7X_LANES), lambda h, i: (0, 0)),
                  pl.BlockSpec((nblk, None, VT_ROWS, bk), lambda h, i: (0, h, 0, 0)),
                  hv, hv],
        out_specs=hv,
        out_shape=jax.ShapeDtypeStruct((S, H * V_HEAD_DIM), BF16),
        scratch_shapes=[pltpu.VMEM((QK_PAD_DIM, bq), BF16),
                        pltpu.VMEM((2, bk, bq), F32),
                        pltpu.VMEM((2, V7X_SUBLANES, bq), F32),
                        pltpu.VMEM((V7X_SUBLANES, bq), F32),
                        pltpu.VMEM((VT_ROWS, bq), F32)],
        compiler_params=_params(("arbitrary", "arbitrary"), pipelined, 6 * _nbytes((bk, bq), F32)),
        name="attention",
    )(q, k_nope, kr, vt, g_attn, conv_term)


def _mm_kernel(x_ref, w_ref, res_ref, o_ref, *, alpha):
    kk = pl.program_id(2)

    @pl.when(kk == 0)
    def _():
        o_ref[...] = alpha * res_ref[...] + jnp.dot(x_ref[...], w_ref[...], preferred_element_type=F32)

    @pl.when(kk > 0)
    def _():
        o_ref[...] += jnp.dot(x_ref[...], w_ref[...], preferred_element_type=F32)


def _matmul_residual(x, w, layer, res, alpha):
    M, K = x.shape
    N = w.shape[-1]
    bm, bn, bk = _blk(M, 1024), _blk(N, 1024), _blk(K, 4096)
    pipelined = _nbytes((bm, bk), BF16) + _nbytes((bk, bn), BF16) + 2 * _nbytes((bm, bn), F32)
    return pl.pallas_call(
        functools.partial(_mm_kernel, alpha=alpha),
        grid=(M // bm, N // bn, K // bk),
        in_specs=[pl.BlockSpec((bm, bk), lambda m, n, k: (m, k)),
                  pl.BlockSpec((None, bk, bn), lambda m, n, k: (layer, k, n)),
                  pl.BlockSpec((bm, bn), lambda m, n, k: (m, n))],
        out_specs=pl.BlockSpec((bm, bn), lambda m, n, k: (m, n)),
        out_shape=jax.ShapeDtypeStruct((M, N), F32),
        compiler_params=_params(("arbitrary", "arbitrary", "arbitrary"), pipelined, _nbytes((bm, bn), F32)),
        name="matmul",
    )(x, w, res)


def _ln_kernel(y_ref, g_ref, b_ref, o32_ref, o16_ref):
    y = y_ref[...]
    mu = jnp.mean(y, axis=-1, keepdims=True)
    d = y - mu
    var = jnp.mean(d * d, axis=-1, keepdims=True)
    out = d * lax.rsqrt(var + LN_EPS) * g_ref[...] + b_ref[...]
    o32_ref[...] = out
    o16_ref[...] = out.astype(BF16)


def _layernorm(y, g, b):
    S, D = y.shape
    bm = _blk(S, 256)
    row = pl.BlockSpec((bm, D), lambda i: (i, 0))
    vec = pl.BlockSpec((1, D), lambda i: (0, 0))
    pipelined = 2 * _nbytes((bm, D), F32) + _nbytes((bm, D), BF16)
    return pl.pallas_call(
        _ln_kernel,
        grid=(S // bm,),
        in_specs=[row, vec, vec],
        out_specs=[row, row],
        out_shape=[jax.ShapeDtypeStruct((S, D), F32), jax.ShapeDtypeStruct((S, D), BF16)],
        compiler_params=_params(("arbitrary",), pipelined, 3 * _nbytes((bm, D), F32)),
        name="layernorm",
    )(y, g.reshape(1, -1), b.reshape(1, -1))


FFN_BN = 512
FFN_BM = 1024


def _ffn_in_kernel(x_ref, wa_ref, wu_ref, cw_ref, o_ref, tail_ref):
    _reset_conv_tail(tail_ref, pl.program_id(1) == 0)
    hm = x_ref.shape[0] // ROW_SPLIT
    for r in range(ROW_SPLIT):
        rows = slice(r * hm, (r + 1) * hm)
        x = x_ref[rows, :]
        a = jnp.dot(x, wa_ref[...], preferred_element_type=F32)
        conv = _causal_conv3(a, cw_ref, tail_ref)
        u = jnp.dot(x, wu_ref[...], preferred_element_type=F32)
        o_ref[rows, :] = (conv * _sigmoid(conv) * u).astype(BF16)


def _ffn_in(xb, w_ffn_in, layer, ffn_conv_w):
    S, D = xb.shape
    FF = ffn_conv_w.shape[-1]
    assert w_ffn_in.shape[1:] == (D, 2 * FF)
    bm = _blk(S, FFN_BM)
    bn = _blk(FF, FFN_BN)
    nt = FF // bn
    pipelined = _nbytes((bm, D), BF16) + 2 * _nbytes((D, bn), BF16) + _nbytes((bm, bn), BF16)
    return pl.pallas_call(
        _ffn_in_kernel,
        grid=(nt, S // bm),
        in_specs=[pl.BlockSpec((bm, D), lambda n, m: (m, 0)),
                  pl.BlockSpec((None, D, bn), lambda n, m: (layer, 0, n)),
                  pl.BlockSpec((None, D, bn), lambda n, m: (layer, 0, nt + n)),
                  pl.BlockSpec((CONV_WIDTH, bn), lambda n, m: (0, n))],
        out_specs=pl.BlockSpec((bm, bn), lambda n, m: (m, n)),
        out_shape=jax.ShapeDtypeStruct((S, FF), BF16),
        scratch_shapes=[pltpu.VMEM((V7X_SUBLANES, bn), F32)],
        compiler_params=_params(("arbitrary", "arbitrary"), pipelined, 6 * _nbytes((bm, bn), F32)),
        name="ffn_in",
    )(xb, w_ffn_in, w_ffn_in, ffn_conv_w)


def kernel(x, positions, w_in, b_gate, q_norm_w, kv_norm_w, w_uq, w_ukv, conv_w, w_o, ln1_g, ln1_b,
           w_ffn_in, ffn_conv_w, w_ffn_down, ln2_g, ln2_b):
    B, S, D = x.shape
    assert B == 1, "kernels are written for a single sequence"
    depth = w_in.shape[0]
    q_lora = q_norm_w.shape[-1]
    kv_lora = kv_norm_w.shape[-1]
    n_lat = q_lora + kv_lora + QK_ROPE_DIM
    assert (q_lora + kv_lora) % V7X_LANES == 0
    H = w_uq.shape[-1] // QK_HEAD_DIM
    alpha = (2.0 * depth) ** 0.25

    w_lat = jnp.pad(w_in[:, :, :n_lat], ((0, 0), (0, 0), (0, V7X_LANES - QK_ROPE_DIM))).astype(BF16)
    w_br = _regroup_cast(jnp.swapaxes(w_in, 1, 2), n_lat, BRANCH_GROUPS, D, _blk(D, BRANCH_BN))
    w_uq_pad = jnp.pad(w_uq.reshape(depth, q_lora, H, QK_HEAD_DIM),
                       ((0, 0), (0, 0), (0, 0), (0, QK_PAD_DIM - QK_HEAD_DIM))
                       ).reshape(depth, q_lora, H * QK_PAD_DIM).astype(BF16)
    w_ukv_b = w_ukv.astype(BF16)
    w_o_b = w_o.astype(BF16)
    w_ffn_in_b = w_ffn_in.astype(BF16)
    w_ffn_down_b = w_ffn_down.astype(BF16)

    tabs = _rope_tables(positions.reshape(S, 1))
    xf = x.reshape(S, D)
    xb = xf.astype(BF16)
    attn_blk = _blk(S, ATTN_BK)
    for l in range(depth):
        qn, ckvn, kr = _latent_proj(xb, w_lat, l, q_norm_w[l], kv_norm_w[l], tabs)
        conv_term, g_attn = _branch_proj(xb, w_br, l, b_gate[l].reshape(1, -1), conv_w[l])
        q = _q_up(qn, w_uq_pad, l, tabs)
        k_nope, vt = _kv_up(ckvn, w_ukv_b, l, attn_blk)
        merged = _attention(q, k_nope, kr, vt, g_attn, conv_term)
        xf, xb = _layernorm(_matmul_residual(merged, w_o_b, l, xf, alpha), ln1_g[l], ln1_b[l])
        g = _ffn_in(xb, w_ffn_in_b, l, ffn_conv_w[l])
        xf, xb = _layernorm(_matmul_residual(g, w_ffn_down_b, l, xf, alpha), ln2_g[l], ln2_b[l])
    return xf.reshape(B, S, D)
```
